```python
import jax, jax.numpy as jnp
from jax import lax
import numpy as np

D_MODEL = 1024
BATCH = 4
SEQ = 4096
DEPTH = 2

CTX_LEN = 256
GRID_W = 64
D_MIX = D_MODEL
D_HGRN = D_MIX // 2
HGRN_HEADS = 4
HGRN_DH = D_HGRN // HGRN_HEADS
D_RWKV = D_MIX - D_HGRN
RWKV_DH = 64
RWKV_HEADS = D_RWKV // RWKV_DH
W_RANK = 64
A_RANK = 64
G_RANK = 128
D_FF = ((8 * D_MODEL // 3 + 255) // 256) * 256
CHUNK = 64
N_MOD = 9
RMS_EPS = 1e-6
GN_EPS = 64e-5
HG_COLS = 5 * D_HGRN
RW_COLS = 3 * D_RWKV + 2 * W_RANK + 2 * A_RANK + G_RANK
N_IN = HG_COLS + RW_COLS
RW_SPLITS = (D_RWKV, 2 * D_RWKV, 3 * D_RWKV, 3 * D_RWKV + W_RANK, 3 * D_RWKV + 2 * W_RANK,
             3 * D_RWKV + 2 * W_RANK + A_RANK, 3 * D_RWKV + 2 * W_RANK + 2 * A_RANK)

kernel_name = 'hybrid_hgrn2_rwkv7_prefix_dit'


def rms_norm(x, gain):
    xf = x.astype(jnp.float32)
    y = xf * lax.rsqrt(jnp.mean(xf * xf, axis=-1, keepdims=True) + RMS_EPS)
    return (y * gain.astype(jnp.float32)).astype(x.dtype)


def modulate(h, shift, scale):
    return h * (1 + scale) + shift


def swiglu(h, w_up, w_down):
    gate, up = jnp.split(h @ w_up, 2, axis=-1)
    return (jax.nn.silu(gate) * up) @ w_down


def ffn_half_step(x, gain, shift, scale, gate, w_up, w_down):
    h = modulate(rms_norm(x, gain), shift, scale)
    return x + 0.5 * gate * swiglu(h, w_up, w_down)


def grid_conv(u, kern):
    b, length, ch = u.shape
    img = u.reshape(b, length // GRID_W, GRID_W, ch)
    out = lax.conv_general_dilated(img, kern[:, :, None, :].astype(u.dtype), (1, 1), 'SAME',
                                   dimension_numbers=('NHWC', 'HWIO', 'NHWC'), feature_group_count=ch)
    return out.reshape(b, length, ch)


def seq_conv(u, taps):
    up = jnp.pad(u, ((0, 0), (1, 1), (0, 0)))
    return up[:, :-2] * taps[0] + up[:, 1:-1] * taps[1] + up[:, 2:] * taps[2]


def flip_segments(u, n_ctx):
    return jnp.concatenate([jnp.flip(u[:, :n_ctx], 1), jnp.flip(u[:, n_ctx:], 1)], axis=1)


def split_heads(u, n_heads):
    n, t, _ = u.shape
    return u.reshape(n, t, n_heads, -1).transpose(0, 2, 1, 3)


def merge_heads(u):
    n, h, t, d = u.shape
    return u.transpose(0, 2, 1, 3).reshape(n, t, h * d)


def head_rms_norm(o, gain, n_heads):
    shp = o.shape
    oh = o.reshape(*shp[:-1], n_heads, -1)
    oh = oh * lax.rsqrt(jnp.mean(oh * oh, axis=-1, keepdims=True) + RMS_EPS)
    return oh.reshape(shp) * gain


def head_group_norm(y, w, b, n_heads):
    shp = y.shape
    yh = y.reshape(*shp[:-1], n_heads, -1)
    mu = jnp.mean(yh, axis=-1, keepdims=True)
    var = jnp.mean(jnp.square(yh - mu), axis=-1, keepdims=True)
    yh = (yh - mu) * lax.rsqrt(var + GN_EPS)
    return yh.reshape(shp) * w + b


def hgrn_forget(z, lb):
    log_f = jnp.logaddexp(jnp.log(lb), jnp.log1p(-lb) + jax.nn.log_sigmoid(z))
    key = (1.0 - lb) * jax.nn.sigmoid(-z)
    return log_f, key


def gla_chunk_scan(q, k, v, log_f):
    n, h, t, dk = q.shape
    dv = v.shape[-1]
    nc = t // CHUNK
    def chunks(u):
        return jnp.moveaxis(u.reshape(n, h, nc, CHUNK, u.shape[-1]), 2, 0)
    causal = jnp.tril(jnp.ones((CHUNK, CHUNK), dtype=bool))[:, :, None]
    def step(state, inp):
        qc, kc, vc, gc = inp
        b = jnp.cumsum(gc, axis=2)
        diff = b[:, :, :, None, :] - b[:, :, None, :, :]
        decay = jnp.exp(jnp.where(causal, diff, -jnp.inf))
        att = jnp.einsum('nhtd,nhsd,nhtsd->nhts', qc, kc, decay)
        out = (jnp.einsum('nhts,nhsv->nhtv', att, vc)
               + jnp.einsum('nhtd,nhdv->nhtv', qc * jnp.exp(b), state))
        b_last = b[:, :, -1:, :]
        state = (state * jnp.exp(b_last)[:, :, 0, :, None]
                 + jnp.einsum('nhsd,nhsv->nhdv', kc * jnp.exp(b_last - b), vc))
        return state, out
    s0 = jnp.zeros((n, h, dk, dv), jnp.float32)
    _, o = lax.scan(step, s0, (chunks(q), chunks(k), chunks(v), chunks(log_f)))
    return jnp.moveaxis(o, 0, 2).reshape(n, h, t, dv)


def rwkv7_scan(r, w, k, v, kk, a):
    n, h, d = r.shape[1:]
    def step(state, inp):
        r_t, w_t, k_t, v_t, kk_t, a_t = inp
        sa = jnp.einsum('nhvk,nhk->nhv', state, -kk_t)
        state = (state * w_t[:, :, None, :] + sa[..., None] * (kk_t * a_t)[:, :, None, :]
                 + v_t[..., None] * k_t[:, :, None, :])
        return state, jnp.einsum('nhvk,nhk->nhv', state, r_t)
    s0 = jnp.zeros((n, h, d, d), jnp.float32)
    _, y = lax.scan(step, s0, (r, w, k, v, kk, a))
    return y


def rwkv_decay(wd, w0, w_up):
    w = -jax.nn.softplus(-(w0 + jnp.tanh(wd) @ w_up)) - 0.5
    return jnp.exp(-jnp.exp(w))


def token_mix(hx, hc, w_in, w_out, lb, o_gain, conv, w0, w_up, a0, a_up, g_up,
              k_k, k_a, r_k, ln_w, ln_b, with_ctx_out):
    bsz = hx.shape[0]
    n_ctx = hc.shape[1]
    f32 = jnp.float32
    px = hx @ w_in
    pc = hc @ w_in
    rw_x = grid_conv(px[..., HG_COLS:], conv)
    rw_c = seq_conv(pc[..., HG_COLS:], conv[1])
    hg = jnp.concatenate([pc[..., :HG_COLS], px[..., :HG_COLS]], axis=1).astype(f32)
    rw = jnp.concatenate([rw_c, rw_x], axis=1).astype(f32)
    t_len = hg.shape[1]
    rev = lambda u: flip_segments(u, n_ctx)
    both = lambda fw, bw: jnp.concatenate([fw, rev(bw)], axis=0)

    q, zf_fw, zf_bw, i_in, g_out = jnp.split(hg, 5, axis=-1)
    q = jax.nn.silu(q)
    lf_fw, k_fw = hgrn_forget(zf_fw, lb[0])
    lf_bw, k_bw = hgrn_forget(zf_bw, lb[1])
    o = gla_chunk_scan(split_heads(both(q, q), HGRN_HEADS), split_heads(both(k_fw, k_bw), HGRN_HEADS),
                       split_heads(both(i_in, i_in), HGRN_HEADS), split_heads(both(lf_fw, lf_bw), HGRN_HEADS))
    o = merge_heads(o)
    o = o[:bsz] + rev(o[bsz:])
    o = head_rms_norm(o, o_gain, HGRN_HEADS) * jax.nn.silu(g_out)

    r, k, v, wd_fw, wd_bw, ad_fw, ad_bw, gd = jnp.split(rw, RW_SPLITS, axis=-1)
    dec_fw = rwkv_decay(wd_fw, w0[0], w_up[0])
    dec_bw = rwkv_decay(wd_bw, w0[1], w_up[1])
    a_fw = jax.nn.sigmoid(a0[0] + ad_fw @ a_up[0])
    a_bw = jax.nn.sigmoid(a0[1] + ad_bw @ a_up[1])
    g_r = jax.nn.sigmoid(gd) @ g_up
    kk = (k * k_k).reshape(bsz, t_len, RWKV_HEADS, RWKV_DH)
    kk = kk / jnp.maximum(jnp.linalg.norm(kk, axis=-1, keepdims=True), 1e-12)
    kk = kk.reshape(bsz, t_len, D_RWKV)
    kf = k * (1 + (a_fw - 1) * k_a)
    kb = k * (1 + (a_bw - 1) * k_a)
    tm = lambda u: split_heads(u, RWKV_HEADS).transpose(2, 0, 1, 3)
    y = rwkv7_scan(tm(both(r, r)), tm(both(dec_fw, dec_bw)), tm(both(kf, kb)),
                   tm(both(v, v)), tm(both(kk, kk)), tm(both(a_fw, a_bw)))
    y = merge_heads(y.transpose(1, 2, 0, 3))
    y = y[:bsz] + rev(y[bsz:])
    y = head_group_norm(y, ln_w, ln_b, RWKV_HEADS)
    bonus = (jnp.sum((r * (kf + kb)).reshape(bsz, t_len, RWKV_HEADS, RWKV_DH) * r_k, axis=-1, keepdims=True)
             * v.reshape(bsz, t_len, RWKV_HEADS, RWKV_DH))
    y = (y + bonus.reshape(bsz, t_len, D_RWKV)) * g_r

    mixed = jnp.concatenate([o, y], axis=-1).astype(hx.dtype)
    out_x = mixed[:, n_ctx:] @ w_out
    out_c = mixed[:, :n_ctx] @ w_out if with_ctx_out else None
    return out_x, out_c


def setup_inputs(seed: int = 0) -> dict:
    key = jax.random.key(seed)
    ks = iter(jax.random.split(key, 32))
    nrm = lambda shape, s: s * jax.random.normal(next(ks), shape, jnp.float32)
    D = D_MODEL
    x = nrm((BATCH, SEQ, D), 1.0)
    c = nrm((BATCH, D), 1.0)
    ctx = nrm((BATCH, CTX_LEN, D), 1.0)
    c_ctx = nrm((D,), 1.0)
    w_mod = nrm((DEPTH, D, N_MOD * D), 0.5 * D ** -0.5)
    b_mod = nrm((DEPTH, N_MOD * D), 0.01)
    norm_gains = 1.0 + nrm((DEPTH, 3, D), 0.05)
    final_gain = 1.0 + nrm((D,), 0.05)
    ffn1_up = nrm((DEPTH, D, 2 * D_FF), D ** -0.5)
    ffn1_down = nrm((DEPTH, D_FF, D), D_FF ** -0.5)
    ffn2_up = nrm((DEPTH, D, 2 * D_FF), D ** -0.5)
    ffn2_down = nrm((DEPTH, D_FF, D), D_FF ** -0.5)
    w_in = nrm((DEPTH, D, N_IN), D ** -0.5)
    w_out = nrm((DEPTH, D_MIX, D), D_MIX ** -0.5)
    hgrn_lb_logits = nrm((DEPTH, 2, D_HGRN), 1.0)
    hgrn_o_gain = 1.0 + nrm((DEPTH, D_HGRN), 0.05)
    rwkv_conv = nrm((DEPTH, 3, 3, RW_COLS), 0.1).at[:, 1, 1].add(1.0)
    rwkv_w0 = jax.random.uniform(next(ks), (DEPTH, 2, D_RWKV), jnp.float32, -6.0, 0.0)
    rwkv_w_up = nrm((DEPTH, 2, W_RANK, D_RWKV), 0.1 * W_RANK ** -0.5)
    rwkv_a0 = nrm((DEPTH, 2, D_RWKV), 0.1)
    rwkv_a_up = nrm((DEPTH, 2, A_RANK, D_RWKV), 0.1 * A_RANK ** -0.5)
    rwkv_g_up = nrm((DEPTH, G_RANK, D_RWKV), G_RANK ** -0.5)
    rwkv_k_k = 0.85 + nrm((DEPTH, D_RWKV), 0.05)
    rwkv_k_a = 1.0 + nrm((DEPTH, D_RWKV), 0.05)
    rwkv_r_k = nrm((DEPTH, RWKV_HEADS, RWKV_DH), 0.1)
    rwkv_ln_w = 1.0 + nrm((DEPTH, D_RWKV), 0.05)
    rwkv_ln_b = nrm((DEPTH, D_RWKV), 0.01)
    return {'x': x, 'c': c, 'ctx': ctx, 'c_ctx': c_ctx, 'w_mod': w_mod, 'b_mod': b_mod,
            'norm_gains': norm_gains, 'final_gain': final_gain,
            'ffn1_up': ffn1_up, 'ffn1_down': ffn1_down, 'ffn2_up': ffn2_up, 'ffn2_down': ffn2_down,
            'w_in': w_in, 'w_out': w_out, 'hgrn_lb_logits': hgrn_lb_logits, 'hgrn_o_gain': hgrn_o_gain,
            'rwkv_conv': rwkv_conv, 'rwkv_w0': rwkv_w0, 'rwkv_w_up': rwkv_w_up, 'rwkv_a0': rwkv_a0,
            'rwkv_a_up': rwkv_a_up, 'rwkv_g_up': rwkv_g_up, 'rwkv_k_k': rwkv_k_k, 'rwkv_k_a': rwkv_k_a,
            'rwkv_r_k': rwkv_r_k, 'rwkv_ln_w': rwkv_ln_w, 'rwkv_ln_b': rwkv_ln_b}


def reference(x, c, ctx, c_ctx, w_mod, b_mod, norm_gains, final_gain, ffn1_up, ffn1_down, ffn2_up,
              ffn2_down, w_in, w_out, hgrn_lb_logits, hgrn_o_gain, rwkv_conv, rwkv_w0, rwkv_w_up,
              rwkv_a0, rwkv_a_up, rwkv_g_up, rwkv_k_k, rwkv_k_a, rwkv_r_k, rwkv_ln_w, rwkv_ln_b):
    p = jax.nn.softmax(hgrn_lb_logits.astype(jnp.float32), axis=0)
    cum = jnp.cumsum(p, axis=0)
    lower_bounds = cum - cum[0]
    sc = jax.nn.silu(c)
    scc = jax.nn.silu(c_ctx)
    for l in range(DEPTH):
        last = l == DEPTH - 1
        mx = jnp.split((sc @ w_mod[l] + b_mod[l])[:, None, :], N_MOD, axis=-1)
        mc = jnp.split(scc @ w_mod[l] + b_mod[l], N_MOD, axis=-1)
        x = ffn_half_step(x, norm_gains[l, 0], mx[0], mx[1], mx[2], ffn1_up[l], ffn1_down[l])
        ctx = ffn_half_step(ctx, norm_gains[l, 0], mc[0], mc[1], mc[2], ffn1_up[l], ffn1_down[l])
        hx = modulate(rms_norm(x, norm_gains[l, 1]), mx[3], mx[4])
        hc = modulate(rms_norm(ctx, norm_gains[l, 1]), mc[3], mc[4])
        mix_x, mix_c = token_mix(hx, hc, w_in[l], w_out[l], lower_bounds[l], hgrn_o_gain[l], rwkv_conv[l],
                                 rwkv_w0[l], rwkv_w_up[l], rwkv_a0[l], rwkv_a_up[l], rwkv_g_up[l],
                                 rwkv_k_k[l], rwkv_k_a[l], rwkv_r_k[l], rwkv_ln_w[l], rwkv_ln_b[l],
                                 not last)
        x = x + mx[5] * mix_x
        x = ffn_half_step(x, norm_gains[l, 2], mx[6], mx[7], mx[8], ffn2_up[l], ffn2_down[l])
        if not last:
            ctx = ctx + mc[5] * mix_c
            ctx = ffn_half_step(ctx, norm_gains[l, 2], mc[6], mc[7], mc[8], ffn2_up[l], ffn2_down[l])
    return rms_norm(x, final_gain)
```

```python
import functools
import math

import numpy as np
import jax
import jax.numpy as jnp
from jax import lax
from jax.experimental import pallas as pl
from jax.experimental.pallas import tpu as pltpu

F32 = jnp.float32
BF16 = jnp.bfloat16

CHUNK = 64
GRID_W = 64
HGRN_HEADS = 4
RWKV_DH = 64
W_RANK = 64
A_RANK = 64
G_RANK = 128
N_MOD = 9
RMS_EPS = 1e-6
GN_EPS = 64e-5
VMEM_LIMIT = 56 * 1024 * 1024


def _cparams(*sem):
    return pltpu.CompilerParams(dimension_semantics=sem, vmem_limit_bytes=VMEM_LIMIT)


def _dot(a, b):
    return jnp.dot(a.astype(BF16), b.astype(BF16), preferred_element_type=F32)


def _dot_nt(a, b):
    return lax.dot_general(a.astype(BF16), b.astype(BF16), (((1,), (1,)), ((), ())),
                           preferred_element_type=F32)


def _dot_tn(a, b):
    return lax.dot_general(a.astype(BF16), b.astype(BF16), (((0,), (0,)), ((), ())),
                           preferred_element_type=F32)


def _split3(x):
    hi = x.astype(BF16)
    r1 = x - hi.astype(F32)
    mid = r1.astype(BF16)
    lo = (r1 - mid.astype(F32)).astype(BF16)
    return hi, mid, lo


def _dot01(m01, x):
    hi, mid, lo = _split3(x)
    d = lambda p: jnp.dot(m01, p, preferred_element_type=F32)
    return d(hi) + d(mid) + d(lo)


def _dot_x01(x, m01):
    hi, mid, lo = _split3(x)
    d = lambda p: jnp.dot(p, m01, preferred_element_type=F32)
    return d(hi) + d(mid) + d(lo)


def _sigmoid(x):
    return 1.0 / (1.0 + jnp.exp(-x))


def _silu(x):
    return x * _sigmoid(x)


def _rms_mod(x, gain, shift, scale):
    y = x * lax.rsqrt(jnp.mean(x * x, axis=-1, keepdims=True) + RMS_EPS)
    return (y * gain) * (1.0 + scale) + shift


def _mod_kernel(c_ref, w_ref, b_ref, o_ref):
    s = _silu(c_ref[...])
    o_ref[0] = jnp.dot(s, w_ref[0], preferred_element_type=F32,
                       precision=lax.Precision.HIGHEST) + b_ref[0]


def _mod_call(cvec, w_mod, b_mod):
    depth, d, n = w_mod.shape
    tn = 1152 if n % 1152 == 0 else n
    return pl.pallas_call(
        _mod_kernel,
        grid=(depth, n // tn),
        in_specs=[pl.BlockSpec((8, d), lambda l, j: (0, 0)),
                  pl.BlockSpec((1, d, tn), lambda l, j: (l, 0, j)),
                  pl.BlockSpec((1, 1, tn), lambda l, j: (l, 0, j))],
        out_specs=pl.BlockSpec((1, 8, tn), lambda l, j: (l, 0, j)),
        out_shape=jax.ShapeDtypeStruct((depth, 8, n), F32),
        compiler_params=_cparams("arbitrary", "arbitrary"),
        name="mod_vectors",
    )(cvec, w_mod, b_mod.reshape(depth, 1, n))


def _row_tile(n_ctx_rows, seq, cap):
    tm = cap
    while n_ctx_rows % tm or seq % tm:
        tm //= 2
    return tm


def _mod_index(i, ctx_tiles, tiles_per_batch, n_batch):
    return jnp.where(i < ctx_tiles, n_batch, (i - ctx_tiles) // tiles_per_batch)


def _resident(shape):
    return pl.BlockSpec(shape, lambda *_: (0,) * len(shape), pipeline_mode=pl.Buffered(1))


def _ffn_kernel(x_ref, m_ref, g_ref, wg_ref, wu_ref, wd_ref, *rest, mod_base, ff_chunk, final):
    if final:
        fg_ref, o_ref = rest
    else:
        (o_ref,) = rest
    x = x_ref[...]
    m = m_ref[0]
    h = _rms_mod(x, g_ref[...], m[mod_base:mod_base + 1], m[mod_base + 1:mod_base + 2]).astype(BF16)
    d_ff = wd_ref.shape[0]
    acc = jnp.zeros(x.shape, F32)
    for j in range(d_ff // ff_chunk):
        sl = slice(j * ff_chunk, (j + 1) * ff_chunk)
        gate = jnp.dot(h, wg_ref[:, sl], preferred_element_type=F32)
        up = jnp.dot(h, wu_ref[:, sl], preferred_element_type=F32)
        act = (_silu(gate) * up).astype(BF16)
        acc = acc + jnp.dot(act, wd_ref[sl, :], preferred_element_type=F32)
    y = x + (0.5 * m[mod_base + 2:mod_base + 3]) * acc
    if final:
        y = (y * lax.rsqrt(jnp.mean(y * y, axis=-1, keepdims=True) + RMS_EPS)) * fg_ref[...]
    o_ref[...] = y


def _ffn_call(xa, mods, gain, wg, wu, wd, *, mod_base, tm, ctx_tiles, tiles_per_batch, n_batch,
              skip_ctx=False, final_gain=None):
    rows, d = xa.shape
    d_ff = wd.shape[0]
    n_tiles = rows // tm
    t0 = ctx_tiles if skip_ctx else 0
    final = final_gain is not None
    in_specs = [pl.BlockSpec((tm, d), lambda i: (i + t0, 0)),
                pl.BlockSpec((1, N_MOD, d),
                             lambda i: (_mod_index(i + t0, ctx_tiles, tiles_per_batch, n_batch), 0, 0)),
                _resident((1, d)), _resident((d, d_ff)), _resident((d, d_ff)), _resident((d_ff, d))]
    args = [xa, mods, gain.reshape(1, d), wg, wu, wd]
    if final:
        in_specs.append(_resident((1, d)))
        args.append(final_gain.reshape(1, d))
    out_rows = rows - t0 * tm
    return pl.pallas_call(
        functools.partial(_ffn_kernel, mod_base=mod_base, ff_chunk=256, final=final),
        grid=(n_tiles - t0,),
        in_specs=in_specs,
        out_specs=pl.BlockSpec((tm, d), lambda i: (i, 0)),
        out_shape=jax.ShapeDtypeStruct((out_rows, d), F32),
        compiler_params=_cparams("arbitrary"),
        name="ffn_half_step",
    )(*args)


def _inproj_kernel(x_ref, m_ref, g_ref, whg_ref, wrw_ref, ohg_ref, orw_ref):
    m = m_ref[0]
    h = _rms_mod(x_ref[...], g_ref[...], m[3:4], m[4:5]).astype(BF16)
    ohg_ref[...] = jnp.dot(h, whg_ref[...], preferred_element_type=F32)
    orw_ref[...] = jnp.dot(h, wrw_ref[...], preferred_element_type=F32)


def _inproj_call(xa, mods, gain, w_hg, w_rw, *, tm, ctx_tiles, tiles_per_batch, n_batch):
    rows, d = xa.shape
    n_hg, n_rw = w_hg.shape[1], w_rw.shape[1]
    return pl.pallas_call(
        _inproj_kernel,
        grid=(rows // tm,),
        in_specs=[pl.BlockSpec((tm, d), lambda i: (i, 0)),
                  pl.BlockSpec((1, N_MOD, d),
                               lambda i: (_mod_index(i, ctx_tiles, tiles_per_batch, n_batch), 0, 0)),
                  _resident((1, d)), _resident((d, n_hg)), _resident((d, n_rw))],
        out_specs=[pl.BlockSpec((tm, n_hg), lambda i: (i, 0)),
                   pl.BlockSpec((tm, n_rw), lambda i: (i, 0))],
        out_shape=[jax.ShapeDtypeStruct((rows, n_hg), F32), jax.ShapeDtypeStruct((rows, n_rw), F32)],
        compiler_params=_cparams("arbitrary"),
        name="in_projection",
    )(xa, mods, gain.reshape(1, d), w_hg, w_rw)


def _chunk_maps(n_batch, nc, nl):
    nt = nc + nl

    def rb(b, c):
        return jnp.where(c < nc, b * nc + c, n_batch * nc + b * nl + (c - nc))

    def c_fwd(i):
        return i

    def c_bwd(i):
        return jnp.where(i < nc, nc - 1 - i, nt + nc - 1 - i)

    def c_prev(c):
        return jnp.where(c < nc, jnp.maximum(c - 1, 0), jnp.maximum(c - 1, nc))

    def c_next(c):
        return jnp.where(c < nc, jnp.minimum(c + 1, nc - 1), jnp.minimum(c + 1, nt - 1))

    return rb, c_fwd, c_bwd, c_prev, c_next


def _tri(rev, strict):
    r = lax.broadcasted_iota(jnp.int32, (CHUNK, CHUNK), 0)
    c = lax.broadcasted_iota(jnp.int32, (CHUNK, CHUNK), 1)
    if rev:
        return (r < c) if strict else (r <= c)
    return (r > c) if strict else (r >= c)


def _gla_tables():
    n = CHUNK
    idx = np.arange(n)
    incl = (idx[:, None] >= idx[None, :])
    rows = [incl, (idx[None, :] > idx[:, None])]
    lq, lk, masks = [], [], [np.eye(n, dtype=bool)]
    h = n // 2
    while h >= 1:
        blk = idx // (2 * h)
        mid = blk * 2 * h + h
        rowpart = (idx % (2 * h)) >= h
        lq.append((idx[None, :] > mid[:, None]) & (idx[None, :] <= idx[:, None]) & rowpart[:, None])
        lk.append((idx[None, :] > idx[:, None]) & (idx[None, :] <= mid[:, None]) & (~rowpart)[:, None])
        masks.append((blk[:, None] == blk[None, :]) & rowpart[:, None] & (~rowpart)[None, :])
        h //= 2
    fwd = np.concatenate(rows + lq + lk, axis=0).astype(np.float32)
    mask_f = np.stack(masks).astype(np.float32)
    n_blocks = fwd.shape[0] // n
    rev = fwd.reshape(n_blocks, n, n)[:, ::-1, ::-1].reshape(fwd.shape)
    mask_r = mask_f[:, ::-1, ::-1]
    return np.stack([fwd, rev]), np.stack([mask_f, mask_r])


def _gla_dir(q_raw, z, v, lbp, tab, masks, s_ref, d):
    n_levels = masks.shape[0] - 1
    log_lb, log_1mlb, one_mlb = lbp[0:1], lbp[1:2], lbp[2:3]
    q = _silu(q_raw)
    log_sig = jnp.minimum(z, 0.0) - jnp.log1p(jnp.exp(-jnp.abs(z)))
    bv = log_1mlb + log_sig
    lf = jnp.maximum(log_lb, bv) + jnp.log1p(jnp.exp(-jnp.abs(log_lb - bv)))
    key = one_mlb * _sigmoid(-z)
    e = _dot01(tab, lf)
    c = CHUNK
    dk = q.shape[1] // HGRN_HEADS
    outs = []
    for h in range(HGRN_HEADS):
        sl = slice(h * dk, (h + 1) * dk)
        qh, kh, vh, eh = q[:, sl], key[:, sl], v[:, sl], e[:, sl]
        b, bt = eh[0:c], eh[c:2 * c]
        att = masks[0] * _dot_nt(qh, kh)
        for lv in range(n_levels):
            eq = eh[(2 + lv) * c:(3 + lv) * c]
            ek = eh[(2 + n_levels + lv) * c:(3 + n_levels + lv) * c]
            att = att + masks[lv + 1] * _dot_nt(qh * jnp.exp(eq), kh * jnp.exp(ek))
        st = s_ref[d, h]
        outs.append(_dot(att, vh) + _dot_nt(qh * jnp.exp(b), st))
        b_tot = b[0:1] if d == 1 else b[c - 1:c]
        s_ref[d, h] = st * jnp.exp(b_tot) + _dot_tn(vh, kh * jnp.exp(bt))
    return jnp.concatenate(outs, axis=1)


def _gla_kernel(qf_ref, zf_ref, vf_ref, qb_ref, zb_ref, vb_ref, lbp_ref, tab_ref, mask_ref,
                of_ref, ob_ref, s_ref):
    @pl.when(pl.program_id(1) == 0)
    def _():
        s_ref[...] = jnp.zeros(s_ref.shape, F32)

    of_ref[...] = _gla_dir(qf_ref[...], zf_ref[...], vf_ref[...], lbp_ref[0], tab_ref[0], mask_ref[0],
                           s_ref, 0)
    ob_ref[...] = _gla_dir(qb_ref[...], zb_ref[...], vb_ref[...], lbp_ref[1], tab_ref[1], mask_ref[1],
                           s_ref, 1)


def _gla_call(p_hg, lbp, *, n_batch, nc, nl):
    rows = p_hg.shape[0]
    dh = p_hg.shape[1] // 5
    rb, c_fwd, c_bwd, _, _ = _chunk_maps(n_batch, nc, nl)
    tab, masks = _gla_tables()
    tab = jnp.asarray(tab, BF16)
    masks = jnp.asarray(masks, F32)

    def spec(col, cmap):
        return pl.BlockSpec((CHUNK, dh), lambda b, i: (rb(b, cmap(i)), col))

    out_f = pl.BlockSpec((CHUNK, dh), lambda b, i: (rb(b, c_fwd(i)), 0))
    out_b = pl.BlockSpec((CHUNK, dh), lambda b, i: (rb(b, c_bwd(i)), 0))
    dk = dh // HGRN_HEADS
    return pl.pallas_call(
        _gla_kernel,
        grid=(n_batch, nc + nl),
        in_specs=[spec(0, c_fwd), spec(1, c_fwd), spec(3, c_fwd),
                  spec(0, c_bwd), spec(2, c_bwd), spec(3, c_bwd),
                  _resident(lbp.shape), _resident(tab.shape), _resident(masks.shape)],
        out_specs=[out_f, out_b],
        out_shape=[jax.ShapeDtypeStruct((rows, dh), F32)] * 2,
        scratch_shapes=[pltpu.VMEM((2, HGRN_HEADS, dk, dk), F32)],
        compiler_params=_cparams("arbitrary", "arbitrary"),
        name="hgrn2_chunk_scan",
    )(p_hg, p_hg, p_hg, p_hg, p_hg, p_hg, lbp, tab, masks)


def _shift_conv(prev, cur, nxt, kern, is_lat, has_prev, has_next):
    n = cur.shape[0]
    row = lax.broadcasted_iota(jnp.int32, cur.shape, 0)
    lat = jnp.where(is_lat, 1.0, 0.0)
    hp = jnp.where(has_prev, 1.0, 0.0)
    hn = jnp.where(has_next, 1.0, 0.0)
    edge_l = (1.0 - lat) * hp
    edge_r = (1.0 - lat) * hn

    def left(x, fill):
        return jnp.where(row == 0, fill, pltpu.roll(x, 1, 0))

    def right(x, fill):
        return jnp.where(row == n - 1, fill, pltpu.roll(x, n - 1, 0))

    out = (left(cur, prev[n - 1:n] * edge_l) * kern[3:4] + cur * kern[4:5]
           + right(cur, nxt[0:1] * edge_r) * kern[5:6])
    zero = jnp.zeros((1, cur.shape[1]), F32)
    up = prev * (lat * hp)
    dn = nxt * (lat * hn)
    out = out + left(up, zero) * kern[0:1] + up * kern[1:2] + right(up, zero) * kern[2:3]
    out = out + left(dn, zero) * kern[6:7] + dn * kern[7:8] + right(dn, zero) * kern[8:9]
    return out


def _rwkv_scan_chunk(r, lw, k, v, kk, a, s_ref, d, rev):
    c = CHUNK
    n_heads = r.shape[1] // RWKV_DH
    tri = _tri(rev, strict=False)
    strict = _tri(rev, strict=True)
    eye = _tri(False, False) & _tri(True, False)
    b_incl = _dot01(jnp.where(tri, 1.0, 0.0).astype(BF16), lw)
    b_excl = b_incl - lw
    b_tot = b_incl[0:1] if rev else b_incl[c - 1:c]
    e_in = jnp.exp(-b_incl)
    e_out = jnp.exp(b_tot - b_incl)
    beta = kk * a
    a_t = -kk * jnp.exp(b_excl)
    r_t = r * jnp.exp(b_incl)
    b_t, k_t = beta * e_in, k * e_in
    b_h, k_h = beta * e_out, k * e_out
    w_tot = jnp.exp(b_tot)
    outs = []
    for h in range(n_heads):
        sl = slice(h * RWKV_DH, (h + 1) * RWKV_DH)
        s0 = s_ref[d, h]
        ar = jnp.concatenate([a_t[:, sl], r_t[:, sl]], axis=0)
        g_b = _dot_nt(ar, b_t[:, sl])
        g_k = _dot_nt(ar, k_t[:, sl])
        a_ab = jnp.where(strict, g_b[0:c], 0.0)
        a_ak = jnp.where(strict, g_k[0:c], 0.0)
        a_rb = jnp.where(tri, g_b[c:], 0.0)
        a_rk = jnp.where(tri, g_k[c:], 0.0)
        p = jnp.where(eye, 1.0, a_ab)
        ak = a_ab
        for _ in range(int(math.log2(c)) - 1):
            ak = _dot(ak, ak)
            p = p + _dot(ak, p)
        vh = v[:, sl]
        u = _dot(p, _dot_nt(a_t[:, sl], s0) + _dot(a_ak, vh))
        outs.append(_dot_nt(r_t[:, sl], s0) + _dot(a_rb, u) + _dot(a_rk, vh))
        s_ref[d, h] = s0 * w_tot[:, sl] + _dot_tn(u, b_h[:, sl]) + _dot_tn(vh, k_h[:, sl])
    return jnp.concatenate(outs, axis=1)


def _rwkv_prep(rw, w0, w_up, a0, a_up, k_k, k_a, bd, d):
    dr = k_k.shape[1]
    r, k, v = rw[:, 0:dr], rw[:, dr:2 * dr], rw[:, 2 * dr:3 * dr]
    o = 3 * dr
    wd = rw[:, o + d * W_RANK:o + (d + 1) * W_RANK]
    o += 2 * W_RANK
    ad = rw[:, o + d * A_RANK:o + (d + 1) * A_RANK]
    x = w0 + _dot(jnp.tanh(wd), w_up)
    lw = -math.exp(-0.5) * _sigmoid(x)
    a = _sigmoid(a0 + _dot(ad, a_up))
    kk = k * k_k
    nrm = jnp.sqrt(_dot_x01(kk * kk, bd))
    kk = kk / jnp.maximum(nrm, 1e-12)
    k_dir = k * (1.0 + (a - 1.0) * k_a)
    return r, lw, k_dir, v, kk, a


def _rwkv_kernel(pf_ref, cf_ref, nf_ref, pb_ref, cb_ref, nb_ref, conv_ref, vec_ref, wup_ref, aup_ref,
                 gup_ref, bd_ref, yf_ref, yb_ref, bonus_ref, gr_ref, s_ref, *, nc, nt):
    i = pl.program_id(1)

    @pl.when(i == 0)
    def _():
        s_ref[...] = jnp.zeros(s_ref.shape, F32)

    kern = conv_ref[...]
    vec = vec_ref[...]
    bd = bd_ref[...]
    dr = bd.shape[0]
    w0, a0 = vec[0:2], vec[2:4]
    k_k, k_a, r_k = vec[4:5], vec[5:6], vec[6:7]

    def flags(c):
        is_lat = c >= nc
        has_prev = jnp.logical_and(c != 0, c != nc)
        has_next = jnp.logical_and(c != nc - 1, c != nt - 1)
        return is_lat, has_prev, has_next

    rw = _shift_conv(pf_ref[...], cf_ref[...], nf_ref[...], kern, *flags(i))
    r, lw, kf, v, kk, a_f = _rwkv_prep(rw, w0[0:1], wup_ref[0], a0[0:1], aup_ref[0], k_k, k_a, bd, 0)
    yf_ref[...] = _rwkv_scan_chunk(r, lw, kf, v, kk, a_f, s_ref, 0, False)
    o = 3 * dr + 2 * W_RANK
    a_b = _sigmoid(a0[1:2] + _dot(rw[:, o + A_RANK:o + 2 * A_RANK], aup_ref[1]))
    k = rw[:, dr:2 * dr]
    k_sum = k * (2.0 + (a_f + a_b - 2.0) * k_a)
    bonus_ref[...] = _dot_x01(r * k_sum * r_k, bd) * v
    gd = rw[:, o + 2 * A_RANK:o + 2 * A_RANK + G_RANK]
    gr_ref[...] = _dot(_sigmoid(gd), gup_ref[...])

    cb = jnp.where(i < nc, nc - 1 - i, nt + nc - 1 - i)
    rw = _shift_conv(pb_ref[...], cb_ref[...], nb_ref[...], kern, *flags(cb))
    r, lw, kb, v, kk, a_b2 = _rwkv_prep(rw, w0[1:2], wup_ref[1], a0[1:2], aup_ref[1], k_k, k_a, bd, 1)
    yb_ref[...] = _rwkv_scan_chunk(r, lw, kb, v, kk, a_b2, s_ref, 1, True)


def _rwkv_call(p_rw, conv, vec, w_up, a_up, g_up, bd, *, n_batch, nc, nl):
    rows, n_rw = p_rw.shape
    dr = bd.shape[0]
    rb, c_fwd, c_bwd, c_prev, c_next = _chunk_maps(n_batch, nc, nl)

    def spec(cmap, nb):
        return pl.BlockSpec((CHUNK, n_rw), lambda b, i: (rb(b, nb(cmap(i))), 0))

    ident = lambda c: c
    out_f = pl.BlockSpec((CHUNK, dr), lambda b, i: (rb(b, c_fwd(i)), 0))
    out_b = pl.BlockSpec((CHUNK, dr), lambda b, i: (rb(b, c_bwd(i)), 0))
    n_heads = dr // RWKV_DH
    return pl.pallas_call(
        functools.partial(_rwkv_kernel, nc=nc, nt=nc + nl),
        grid=(n_batch, nc + nl),
        in_specs=[spec(c_fwd, c_prev), spec(c_fwd, ident), spec(c_fwd, c_next),
                  spec(c_bwd, c_prev), spec(c_bwd, ident), spec(c_bwd, c_next),
                  _resident(conv.shape), _resident(vec.shape), _resident(w_up.shape),
                  _resident(a_up.shape), _resident(g_up.shape), _resident(bd.shape)],
        out_specs=[out_f, out_b, out_f, out_f],
        out_shape=[jax.ShapeDtypeStruct((rows, dr), F32)] * 4,
        scratch_shapes=[pltpu.VMEM((2, n_heads, RWKV_DH, RWKV_DH), F32)],
        compiler_params=_cparams("arbitrary", "arbitrary"),
        name="rwkv7_chunk_scan",
    )(p_rw, p_rw, p_rw, p_rw, p_rw, p_rw, conv, vec, w_up, a_up, g_up, bd)


def _post_kernel(x_ref, m_ref, of_ref, ob_ref, go_ref, yf_ref, yb_ref, bonus_ref, gr_ref, vec_ref,
                 bd_ref, wo_ref, o_ref):
    vec = vec_ref[...]
    o_gain, ln_w, ln_b = vec[0:1], vec[1:2], vec[2:3]
    o = of_ref[...] + ob_ref[...]
    dh = o.shape[1] // HGRN_HEADS
    parts = []
    for h in range(HGRN_HEADS):
        oh = o[:, h * dh:(h + 1) * dh]
        parts.append(oh * lax.rsqrt(jnp.mean(oh * oh, axis=-1, keepdims=True) + RMS_EPS))
    o = jnp.concatenate(parts, axis=1) * o_gain * _silu(go_ref[...])
    y = yf_ref[...] + yb_ref[...]
    bd = bd_ref[...]
    inv = 1.0 / RWKV_DH
    mu = _dot_x01(y, bd) * inv
    yc = y - mu
    var = _dot_x01(yc * yc, bd) * inv
    y = yc * lax.rsqrt(var + GN_EPS) * ln_w + ln_b
    y = (y + bonus_ref[...]) * gr_ref[...]
    mixed = jnp.concatenate([o, y], axis=1).astype(BF16)
    out = jnp.dot(mixed, wo_ref[...], preferred_element_type=F32)
    o_ref[...] = x_ref[...] + m_ref[0][5:6] * out


def _post_call(xa, mods, o_f, o_b, p_hg, y_f, y_b, bonus, g_r, vec, bd, w_out, *, tm, ctx_tiles,
               tiles_per_batch, n_batch):
    rows, d = xa.shape
    dh = o_f.shape[1]
    dr = y_f.shape[1]
    row = lambda w: pl.BlockSpec((tm, w), lambda i: (i, 0))
    return pl.pallas_call(
        _post_kernel,
        grid=(rows // tm,),
        in_specs=[row(d),
                  pl.BlockSpec((1, N_MOD, d),
                               lambda i: (_mod_index(i, ctx_tiles, tiles_per_batch, n_batch), 0, 0)),
                  row(dh), row(dh), pl.BlockSpec((tm, dh), lambda i: (i, 4)),
                  row(dr), row(dr), row(dr), row(dr),
                  _resident(vec.shape), _resident(bd.shape), _resident(w_out.shape)],
        out_specs=row(d),
        out_shape=jax.ShapeDtypeStruct((rows, d), F32),
        compiler_params=_cparams("arbitrary"),
        name="mix_out_projection",
    )(xa, mods, o_f, o_b, p_hg, y_f, y_b, bonus, g_r, vec, bd, w_out)


def kernel(x, c, ctx, c_ctx, w_mod, b_mod, norm_gains, final_gain, ffn1_up, ffn1_down, ffn2_up, ffn2_down, w_in, w_out, hgrn_lb_logits, hgrn_o_gain, rwkv_conv, rwkv_w0, rwkv_w_up, rwkv_a0, rwkv_a_up, rwkv_g_up, rwkv_k_k, rwkv_k_a, rwkv_r_k, rwkv_ln_w, rwkv_ln_b):
    n_batch, seq, d = x.shape
    n_ctx = ctx.shape[1]
    depth = w_mod.shape[0]
    d_ff = ffn1_down.shape[1]
    d_hgrn = hgrn_o_gain.shape[1]
    d_rwkv = rwkv_k_k.shape[1]
    hg_cols = 5 * d_hgrn
    assert seq % CHUNK == 0 and n_ctx % CHUNK == 0 and GRID_W == CHUNK and n_batch < 8
    nc, nl = n_ctx // CHUNK, seq // CHUNK

    p = jax.nn.softmax(hgrn_lb_logits.astype(F32), axis=0)
    cum = jnp.cumsum(p, axis=0)
    lb = cum - cum[0]
    lbp = jnp.stack([jnp.log(lb), jnp.log1p(-lb), 1.0 - lb], axis=2)

    cvec = jnp.concatenate([c, c_ctx[None], jnp.zeros((7 - n_batch, d), F32)], axis=0)
    mods = _mod_call(cvec, w_mod, b_mod).reshape(depth, 8, N_MOD, d)

    bd = jnp.asarray(np.kron(np.eye(d_rwkv // RWKV_DH), np.ones((RWKV_DH, RWKV_DH))), BF16)

    xa = jnp.concatenate([ctx.reshape(n_batch * n_ctx, d), x.reshape(n_batch * seq, d)], axis=0)
    tm = _row_tile(n_batch * n_ctx, seq, 1024)
    tm_s = _row_tile(n_batch * n_ctx, seq, 512)
    tiles = dict(tm=tm, ctx_tiles=n_batch * n_ctx // tm, tiles_per_batch=seq // tm, n_batch=n_batch)
    tiles_s = dict(tm=tm_s, ctx_tiles=n_batch * n_ctx // tm_s, tiles_per_batch=seq // tm_s, n_batch=n_batch)

    for l in range(depth):
        last = l == depth - 1
        bf = lambda w: w.astype(BF16)
        xa = _ffn_call(xa, mods[l], norm_gains[l, 0], bf(ffn1_up[l, :, :d_ff]), bf(ffn1_up[l, :, d_ff:]),
                       bf(ffn1_down[l]), mod_base=0, **tiles)
        p_hg, p_rw = _inproj_call(xa, mods[l], norm_gains[l, 1], bf(w_in[l, :, :hg_cols]),
                                  bf(w_in[l, :, hg_cols:]), **tiles_s)
        o_f, o_b = _gla_call(p_hg, lbp[l], n_batch=n_batch, nc=nc, nl=nl)
        vec = jnp.concatenate([rwkv_w0[l], rwkv_a0[l], rwkv_k_k[l][None], rwkv_k_a[l][None],
                               rwkv_r_k[l].reshape(1, d_rwkv), jnp.zeros((1, d_rwkv), F32)], axis=0)
        y_f, y_b, bonus, g_r = _rwkv_call(p_rw, rwkv_conv[l].reshape(9, -1), vec, bf(rwkv_w_up[l]),
                                          bf(rwkv_a_up[l]), bf(rwkv_g_up[l]), bd,
                                          n_batch=n_batch, nc=nc, nl=nl)
        vec2 = jnp.concatenate([hgrn_o_gain[l][None], rwkv_ln_w[l][None], rwkv_ln_b[l][None],
                                jnp.zeros((5, d_rwkv), F32)], axis=0)
        xa = _post_call(xa, mods[l], o_f, o_b, p_hg, y_f, y_b, bonus, g_r, vec2, bd, bf(w_out[l]),
                        **tiles_s)
        up, down = bf(ffn2_up[l]), bf(ffn2_down[l])
        xa = _ffn_call(xa, mods[l], norm_gains[l, 2], up[:, :d_ff], up[:, d_ff:], down, mod_base=6,
                       skip_ctx=last, final_gain=final_gain if last else None, **tiles)
    return xa.reshape(n_batch, seq, d)
```

```python
import functools
import math

import numpy as np
import jax
import jax.numpy as jnp
from jax import lax
from jax.experimental import pallas as pl
from jax.experimental.pallas import tpu as pltpu

F32 = jnp.float32
BF16 = jnp.bfloat16

CHUNK = 64
GRID_W = 64
HGRN_HEADS = 4
RWKV_DH = 64
W_RANK = 64
A_RANK = 64
G_RANK = 128
N_MOD = 9
RMS_EPS = 1e-6
GN_EPS = 64e-5
VMEM_LIMIT = 56 * 1024 * 1024


def _cparams(*sem):
    return pltpu.CompilerParams(dimension_semantics=sem, vmem_limit_bytes=VMEM_LIMIT)


def _dot(a, b):
    return jnp.dot(a.astype(BF16), b.astype(BF16), preferred_element_type=F32)


def _dot_nt(a, b):
    return lax.dot_general(a.astype(BF16), b.astype(BF16), (((1,), (1,)), ((), ())),
                           preferred_element_type=F32)


def _dot_tn(a, b):
    return lax.dot_general(a.astype(BF16), b.astype(BF16), (((0,), (0,)), ((), ())),
                           preferred_element_type=F32)


def _split3(x):
    hi = x.astype(BF16)
    r1 = x - hi.astype(F32)
    mid = r1.astype(BF16)
    lo = (r1 - mid.astype(F32)).astype(BF16)
    return hi, mid, lo


def _dot01(m01, x):
    hi, mid, lo = _split3(x)
    d = lambda p: jnp.dot(m01, p, preferred_element_type=F32)
    return d(hi) + d(mid) + d(lo)


def _dot_x01(x, m01):
    hi, mid, lo = _split3(x)
    d = lambda p: jnp.dot(p, m01, preferred_element_type=F32)
    return d(hi) + d(mid) + d(lo)


def _sigmoid(x):
    return 1.0 / (1.0 + jnp.exp(-x))


def _silu(x):
    return x * _sigmoid(x)


def _rms_mod(x, gain, shift, scale):
    y = x * lax.rsqrt(jnp.mean(x * x, axis=-1, keepdims=True) + RMS_EPS)
    return (y * gain) * (1.0 + scale) + shift


def _mod_kernel(c_ref, w_ref, b_ref, o_ref):
    s = _silu(c_ref[...])
    o_ref[0] = jnp.dot(s, w_ref[0], preferred_element_type=F32,
                       precision=lax.Precision.HIGHEST) + b_ref[0]


def _mod_call(cvec, w_mod, b_mod):
    depth, d, n = w_mod.shape
    tn = 1152 if n % 1152 == 0 else n
    return pl.pallas_call(
        _mod_kernel,
        grid=(depth, n // tn),
        in_specs=[pl.BlockSpec((8, d), lambda l, j: (0, 0)),
                  pl.BlockSpec((1, d, tn), lambda l, j: (l, 0, j)),
                  pl.BlockSpec((1, 1, tn), lambda l, j: (l, 0, j))],
        out_specs=pl.BlockSpec((1, 8, tn), lambda l, j: (l, 0, j)),
        out_shape=jax.ShapeDtypeStruct((depth, 8, n), F32),
        compiler_params=_cparams("arbitrary", "arbitrary"),
        name="mod_vectors",
    )(cvec, w_mod, b_mod.reshape(depth, 1, n))


def _row_tile(n_ctx_rows, seq, cap):
    tm = cap
    while n_ctx_rows % tm or seq % tm:
        tm //= 2
    return tm


def _mod_index(i, ctx_tiles, tiles_per_batch, n_batch):
    return jnp.where(i < ctx_tiles, n_batch, (i - ctx_tiles) // tiles_per_batch)


def _resident(shape):
    return pl.BlockSpec(shape, lambda *_: (0,) * len(shape), pipeline_mode=pl.Buffered(1))


def _ffn_kernel(x_ref, m_ref, g_ref, wg_ref, wu_ref, wd_ref, *rest, mod_base, ff_chunk, final):
    if final:
        fg_ref, o_ref = rest
    else:
        (o_ref,) = rest
    x = x_ref[...]
    m = m_ref[0]
    h = _rms_mod(x, g_ref[...], m[mod_base:mod_base + 1], m[mod_base + 1:mod_base + 2]).astype(BF16)
    d_ff = wd_ref.shape[0]
    acc = jnp.zeros(x.shape, F32)
    for j in range(d_ff // ff_chunk):
        sl = slice(j * ff_chunk, (j + 1) * ff_chunk)
        gate = jnp.dot(h, wg_ref[:, sl], preferred_element_type=F32)
        up = jnp.dot(h, wu_ref[:, sl], preferred_element_type=F32)
        act = (_silu(gate) * up).astype(BF16)
        acc = acc + jnp.dot(act, wd_ref[sl, :], preferred_element_type=F32)
    y = x + (0.5 * m[mod_base + 2:mod_base + 3]) * acc
    if final:
        y = (y * lax.rsqrt(jnp.mean(y * y, axis=-1, keepdims=True) + RMS_EPS)) * fg_ref[...]
    o_ref[...] = y


def _ffn_call(xa, mods, gain, wg, wu, wd, *, mod_base, tm, ctx_tiles, tiles_per_batch, n_batch,
              skip_ctx=False, final_gain=None):
    rows, d = xa.shape
    d_ff = wd.shape[0]
    n_tiles = rows // tm
    t0 = ctx_tiles if skip_ctx else 0
    final = final_gain is not None
    in_specs = [pl.BlockSpec((tm, d), lambda i: (i + t0, 0)),
                pl.BlockSpec((1, N_MOD, d),
                             lambda i: (_mod_index(i + t0, ctx_tiles, tiles_per_batch, n_batch), 0, 0)),
                _resident((1, d)), _resident((d, d_ff)), _resident((d, d_ff)), _resident((d_ff, d))]
    args = [xa, mods, gain.reshape(1, d), wg, wu, wd]
    if final:
        in_specs.append(_resident((1, d)))
        args.append(final_gain.reshape(1, d))
    out_rows = rows - t0 * tm
    return pl.pallas_call(
        functools.partial(_ffn_kernel, mod_base=mod_base, ff_chunk=256, final=final),
        grid=(n_tiles - t0,),
        in_specs=in_specs,
        out_specs=pl.BlockSpec((tm, d), lambda i: (i, 0)),
        out_shape=jax.ShapeDtypeStruct((out_rows, d), F32),
        compiler_params=_cparams("arbitrary"),
        name="ffn_half_step",
    )(*args)


def _inproj_kernel(x_ref, m_ref, g_ref, whg_ref, wrw_ref, ohg_ref, orw_ref):
    m = m_ref[0]
    h = _rms_mod(x_ref[...], g_ref[...], m[3:4], m[4:5]).astype(BF16)
    ohg_ref[...] = jnp.dot(h, whg_ref[...], preferred_element_type=F32)
    orw_ref[...] = jnp.dot(h, wrw_ref[...], preferred_element_type=F32)


def _inproj_call(xa, mods, gain, w_hg, w_rw, *, tm, ctx_tiles, tiles_per_batch, n_batch):
    rows, d = xa.shape
    n_hg, n_rw = w_hg.shape[1], w_rw.shape[1]
    return pl.pallas_call(
        _inproj_kernel,
        grid=(rows // tm,),
        in_specs=[pl.BlockSpec((tm, d), lambda i: (i, 0)),
                  pl.BlockSpec((1, N_MOD, d),
                               lambda i: (_mod_index(i, ctx_tiles, tiles_per_batch, n_batch), 0, 0)),
                  _resident((1, d)), _resident((d, n_hg)), _resident((d, n_rw))],
        out_specs=[pl.BlockSpec((tm, n_hg), lambda i: (i, 0)),
                   pl.BlockSpec((tm, n_rw), lambda i: (i, 0))],
        out_shape=[jax.ShapeDtypeStruct((rows, n_hg), F32), jax.ShapeDtypeStruct((rows, n_rw), F32)],
        compiler_params=_cparams("arbitrary"),
        name="in_projection",
    )(xa, mods, gain.reshape(1, d), w_hg, w_rw)


def _chunk_maps(n_batch, nc, nl):
    nt = nc + nl

    def rb(b, c):
        return jnp.where(c < nc, b * nc + c, n_batch * nc + b * nl + (c - nc))

    def c_fwd(i):
        return i

    def c_bwd(i):
        return jnp.where(i < nc, nc - 1 - i, nt + nc - 1 - i)

    def c_prev(c):
        return jnp.where(c < nc, jnp.maximum(c - 1, 0), jnp.maximum(c - 1, nc))

    def c_next(c):
        return jnp.where(c < nc, jnp.minimum(c + 1, nc - 1), jnp.minimum(c + 1, nt - 1))

    return rb, c_fwd, c_bwd, c_prev, c_next


def _tri(rev, strict):
    r = lax.broadcasted_iota(jnp.int32, (CHUNK, CHUNK), 0)
    c = lax.broadcasted_iota(jnp.int32, (CHUNK, CHUNK), 1)
    if rev:
        return (r < c) if strict else (r <= c)
    return (r > c) if strict else (r >= c)


def _gla_tables():
    n = CHUNK
    idx = np.arange(n)
    incl = (idx[:, None] >= idx[None, :])
    rows = [incl, (idx[None, :] > idx[:, None])]
    lq, lk, masks = [], [], [np.eye(n, dtype=bool)]
    h = n // 2
    while h >= 1:
        blk = idx // (2 * h)
        mid = blk * 2 * h + h
        rowpart = (idx % (2 * h)) >= h
        lq.append((idx[None, :] > mid[:, None]) & (idx[None, :] <= idx[:, None]) & rowpart[:, None])
        lk.append((idx[None, :] > idx[:, None]) & (idx[None, :] <= mid[:, None]) & (~rowpart)[:, None])
        masks.append((blk[:, None] == blk[None, :]) & rowpart[:, None] & (~rowpart)[None, :])
        h //= 2
    fwd = np.concatenate(rows + lq + lk, axis=0).astype(np.float32)
    mask_f = np.stack(masks).astype(np.float32)
    n_blocks = fwd.shape[0] // n
    rev = fwd.reshape(n_blocks, n, n)[:, ::-1, ::-1].reshape(fwd.shape)
    mask_r = mask_f[:, ::-1, ::-1]
    return np.stack([fwd, rev]), np.stack([mask_f, mask_r])


def _gla_dir(q_raw, z, v, lbp, tab, masks, s_ref, d):
    n_levels = masks.shape[0] - 1
    log_lb, log_1mlb, one_mlb = lbp[0:1], lbp[1:2], lbp[2:3]
    q = _silu(q_raw)
    log_sig = jnp.minimum(z, 0.0) - jnp.log1p(jnp.exp(-jnp.abs(z)))
    bv = log_1mlb + log_sig
    lf = jnp.maximum(log_lb, bv) + jnp.log1p(jnp.exp(-jnp.abs(log_lb - bv)))
    key = one_mlb * _sigmoid(-z)
    e = _dot01(tab, lf)
    c = CHUNK
    dk = q.shape[1] // HGRN_HEADS
    outs = []
    for h in range(HGRN_HEADS):
        sl = slice(h * dk, (h + 1) * dk)
        qh, kh, vh, eh = q[:, sl], key[:, sl], v[:, sl], e[:, sl]
        b, bt = eh[0:c], eh[c:2 * c]
        att = masks[0] * _dot_nt(qh, kh)
        for lv in range(n_levels):
            eq = eh[(2 + lv) * c:(3 + lv) * c]
            ek = eh[(2 + n_levels + lv) * c:(3 + n_levels + lv) * c]
            att = att + masks[lv + 1] * _dot_nt(qh * jnp.exp(eq), kh * jnp.exp(ek))
        st = s_ref[d, h]
        outs.append(_dot(att, vh) + _dot_nt(qh * jnp.exp(b), st))
        b_tot = b[0:1] if d == 1 else b[c - 1:c]
        s_ref[d, h] = st * jnp.exp(b_tot) + _dot_tn(vh, kh * jnp.exp(bt))
    return jnp.concatenate(outs, axis=1)


def _gla_kernel(qf_ref, zf_ref, vf_ref, qb_ref, zb_ref, vb_ref, lbp_ref, tab_ref, mask_ref,
                of_ref, ob_ref, s_ref):
    @pl.when(pl.program_id(1) == 0)
    def _():
        s_ref[...] = jnp.zeros(s_ref.shape, F32)

    of_ref[...] = _gla_dir(qf_ref[...], zf_ref[...], vf_ref[...], lbp_ref[0], tab_ref[0], mask_ref[0],
                           s_ref, 0)
    ob_ref[...] = _gla_dir(qb_ref[...], zb_ref[...], vb_ref[...], lbp_ref[1], tab_ref[1], mask_ref[1],
                           s_ref, 1)


def _gla_call(p_hg, lbp, *, n_batch, nc, nl):
    rows = p_hg.shape[0]
    dh = p_hg.shape[1] // 5
    rb, c_fwd, c_bwd, _, _ = _chunk_maps(n_batch, nc, nl)
    tab, masks = _gla_tables()
    tab = jnp.asarray(tab, BF16)
    masks = jnp.asarray(masks, F32)

    def spec(col, cmap):
        return pl.BlockSpec((CHUNK, dh), lambda b, i: (rb(b, cmap(i)), col))

    out_f = pl.BlockSpec((CHUNK, dh), lambda b, i: (rb(b, c_fwd(i)), 0))
    out_b = pl.BlockSpec((CHUNK, dh), lambda b, i: (rb(b, c_bwd(i)), 0))
    dk = dh // HGRN_HEADS
    return pl.pallas_call(
        _gla_kernel,
        grid=(n_batch, nc + nl),
        in_specs=[spec(0, c_fwd), spec(1, c_fwd), spec(3, c_fwd),
                  spec(0, c_bwd), spec(2, c_bwd), spec(3, c_bwd),
                  _resident(lbp.shape), _resident(tab.shape), _resident(masks.shape)],
        out_specs=[out_f, out_b],
        out_shape=[jax.ShapeDtypeStruct((rows, dh), F32)] * 2,
        scratch_shapes=[pltpu.VMEM((2, HGRN_HEADS, dk, dk), F32)],
        compiler_params=_cparams("arbitrary", "arbitrary"),
        name="hgrn2_chunk_scan",
    )(p_hg, p_hg, p_hg, p_hg, p_hg, p_hg, lbp, tab, masks)


def _shift_conv(prev, cur, nxt, kern, is_lat, has_prev, has_next):
    n = cur.shape[0]
    row = lax.broadcasted_iota(jnp.int32, cur.shape, 0)
    lat = jnp.where(is_lat, 1.0, 0.0)
    hp = jnp.where(has_prev, 1.0, 0.0)
    hn = jnp.where(has_next, 1.0, 0.0)
    edge_l = (1.0 - lat) * hp
    edge_r = (1.0 - lat) * hn

    def left(x, fill):
        return jnp.where(row == 0, fill, pltpu.roll(x, 1, 0))

    def right(x, fill):
        return jnp.where(row == n - 1, fill, pltpu.roll(x, n - 1, 0))

    out = (left(cur, prev[n - 1:n] * edge_l) * kern[3:4] + cur * kern[4:5]
           + right(cur, nxt[0:1] * edge_r) * kern[5:6])
    zero = jnp.zeros((1, cur.shape[1]), F32)
    up = prev * (lat * hp)
    dn = nxt * (lat * hn)
    out = out + left(up, zero) * kern[0:1] + up * kern[1:2] + right(up, zero) * kern[2:3]
    out = out + left(dn, zero) * kern[6:7] + dn * kern[7:8] + right(dn, zero) * kern[8:9]
    return out


def _pair_diag(x):
    lo = lax.broadcasted_iota(jnp.int32, x.shape, 1) < RWKV_DH
    z = jnp.zeros_like(x)
    return jnp.concatenate([jnp.where(lo, x, z), jnp.where(lo, z, x)], axis=0)


def _rwkv_scan_chunks(dirs, s_ref):
    c = CHUNK
    pw = 2 * RWKV_DH
    n_pairs = dirs[0][0].shape[1] // pw
    row = lax.broadcasted_iota(jnp.int32, (c, pw), 0)
    lane = lax.broadcasted_iota(jnp.int32, (c, pw), 1)
    col = lane & (RWKV_DH - 1)
    lo = lane < RWKV_DH
    eye = row == col
    bf = lambda t: t.astype(BF16)
    mm = lambda a, b: jnp.dot(a, b, preferred_element_type=F32)
    mm_nt = lambda a, b: lax.dot_general(a, b, (((1,), (1,)), ((), ())), preferred_element_type=F32)
    mm_tn = lambda a, b: lax.dot_general(a, b, (((0,), (0,)), ((), ())), preferred_element_type=F32)

    chains = []
    for d, (r, lw, k, v, kk, a) in enumerate(dirs):
        rev = d == 1
        tri = (row <= col) if rev else (row >= col)
        strict = (row < col) if rev else (row > col)
        b_incl = _dot01(jnp.where(_tri(rev, False), 1.0, 0.0).astype(BF16), lw)
        b_excl = b_incl - lw
        b_tot = b_incl[0:1] if rev else b_incl[c - 1:c]
        e_in = jnp.exp(-b_incl)
        e_out = jnp.exp(b_tot - b_incl)
        beta = kk * a
        a_t = -kk * jnp.exp(b_excl)
        r_t = r * jnp.exp(b_incl)
        b_t, k_t = beta * e_in, k * e_in
        b_h, k_h = beta * e_out, k * e_out
        w_tot = jnp.exp(b_tot)
        for j in range(n_pairs):
            sl = slice(j * pw, (j + 1) * pw)
            chains.append(dict(
                d=d, j=j, tri=tri, strict=strict, w_tot=w_tot[:, sl], v=v[:, sl],
                ar=bf(jnp.concatenate([a_t[:, sl], r_t[:, sl]], axis=0)),
                bt=_pair_diag(bf(b_t[:, sl])), kt=_pair_diag(bf(k_t[:, sl])),
                bk=bf(jnp.concatenate([b_h[:, sl], k_h[:, sl]], axis=0))))

    for ch in chains:
        g_b = mm_nt(ch["ar"], ch["bt"])
        g_k = mm_nt(ch["ar"], ch["kt"])
        a_ab = jnp.where(ch["strict"], g_b[0:c], 0.0)
        ch["a_rb"] = bf(jnp.where(ch["tri"], g_b[c:], 0.0))
        ch["a_k"] = bf(jnp.concatenate([jnp.where(ch["strict"], g_k[0:c], 0.0),
                                        jnp.where(ch["tri"], g_k[c:], 0.0)], axis=0))
        ch["p"] = jnp.where(eye, 1.0, a_ab)
        ch["ak"] = bf(a_ab)
    for ch in chains:
        ch["ak2"] = mm(ch["ak"], _pair_diag(ch["ak"]))
    n_lv = int(math.log2(c))
    for lv in range(2, n_lv + 1):
        for ch in chains:
            ak = bf(ch["ak2"])
            p_bd = _pair_diag(bf(ch["p"]))
            if lv < n_lv:
                both = mm(ak, jnp.concatenate([_pair_diag(ak), p_bd], axis=1))
                ch["ak2"] = both[:, :pw]
                ch["p"] = ch["p"] + both[:, pw:]
            else:
                ch["p"] = ch["p"] + mm(ak, p_bd)
    for ch in chains:
        s0 = s_ref[ch["d"], ch["j"]]
        ch["s0"] = s0
        ch["xs"] = mm_nt(ch["ar"], _pair_diag(bf(s0)))
        ch["av"] = mm(ch["a_k"], _pair_diag(bf(ch["v"])))
    for ch in chains:
        x = ch["xs"][0:c] + ch["av"][0:c]
        ch["u"] = mm(bf(ch["p"]), _pair_diag(bf(x)))
    outs = [[None] * n_pairs for _ in dirs]
    for ch in chains:
        u = ch["u"]
        outs[ch["d"]][ch["j"]] = ch["xs"][c:] + ch["av"][c:] + mm(ch["a_rb"], _pair_diag(bf(u)))
        z = mm_tn(bf(jnp.concatenate([u, ch["v"]], axis=0)), ch["bk"])
        s_ref[ch["d"], ch["j"]] = ch["s0"] * ch["w_tot"] + jnp.where(lo, z[0:c], z[c:])
    return [jnp.concatenate(o, axis=1) for o in outs]


def _rwkv_prep(rw, w0, w_up, a0, a_up, k_k, k_a, bd, d):
    dr = k_k.shape[1]
    r, k, v = rw[:, 0:dr], rw[:, dr:2 * dr], rw[:, 2 * dr:3 * dr]
    o = 3 * dr
    wd = rw[:, o + d * W_RANK:o + (d + 1) * W_RANK]
    o += 2 * W_RANK
    ad = rw[:, o + d * A_RANK:o + (d + 1) * A_RANK]
    x = w0 + _dot(jnp.tanh(wd), w_up)
    lw = -math.exp(-0.5) * _sigmoid(x)
    a = _sigmoid(a0 + _dot(ad, a_up))
    kk = k * k_k
    nrm = jnp.sqrt(_dot_x01(kk * kk, bd))
    kk = kk / jnp.maximum(nrm, 1e-12)
    k_dir = k * (1.0 + (a - 1.0) * k_a)
    return r, lw, k_dir, v, kk, a


def _rwkv_kernel(pf_ref, cf_ref, nf_ref, pb_ref, cb_ref, nb_ref, conv_ref, vec_ref, wup_ref, aup_ref,
                 gup_ref, bd_ref, yf_ref, yb_ref, bonus_ref, gr_ref, s_ref, *, nc, nt):
    i = pl.program_id(1)

    @pl.when(i == 0)
    def _():
        s_ref[...] = jnp.zeros(s_ref.shape, F32)

    kern = conv_ref[...]
    vec = vec_ref[...]
    bd = bd_ref[...]
    dr = bd.shape[0]
    w0, a0 = vec[0:2], vec[2:4]
    k_k, k_a, r_k = vec[4:5], vec[5:6], vec[6:7]

    def flags(c):
        is_lat = c >= nc
        has_prev = jnp.logical_and(c != 0, c != nc)
        has_next = jnp.logical_and(c != nc - 1, c != nt - 1)
        return is_lat, has_prev, has_next

    rw = _shift_conv(pf_ref[...], cf_ref[...], nf_ref[...], kern, *flags(i))
    fwd = _rwkv_prep(rw, w0[0:1], wup_ref[0], a0[0:1], aup_ref[0], k_k, k_a, bd, 0)
    r, _, _, v, _, a_f = fwd
    o = 3 * dr + 2 * W_RANK
    a_b = _sigmoid(a0[1:2] + _dot(rw[:, o + A_RANK:o + 2 * A_RANK], aup_ref[1]))
    k = rw[:, dr:2 * dr]
    k_sum = k * (2.0 + (a_f + a_b - 2.0) * k_a)
    bonus_ref[...] = _dot_x01(r * k_sum * r_k, bd) * v
    gd = rw[:, o + 2 * A_RANK:o + 2 * A_RANK + G_RANK]
    gr_ref[...] = _dot(_sigmoid(gd), gup_ref[...])

    cb = jnp.where(i < nc, nc - 1 - i, nt + nc - 1 - i)
    rw = _shift_conv(pb_ref[...], cb_ref[...], nb_ref[...], kern, *flags(cb))
    bwd = _rwkv_prep(rw, w0[1:2], wup_ref[1], a0[1:2], aup_ref[1], k_k, k_a, bd, 1)
    y_f, y_b = _rwkv_scan_chunks([fwd, bwd], s_ref)
    yf_ref[...] = y_f
    yb_ref[...] = y_b


def _rwkv_call(p_rw, conv, vec, w_up, a_up, g_up, bd, *, n_batch, nc, nl):
    rows, n_rw = p_rw.shape
    dr = bd.shape[0]
    rb, c_fwd, c_bwd, c_prev, c_next = _chunk_maps(n_batch, nc, nl)

    def spec(cmap, nb):
        return pl.BlockSpec((CHUNK, n_rw), lambda b, i: (rb(b, nb(cmap(i))), 0))

    ident = lambda c: c
    out_f = pl.BlockSpec((CHUNK, dr), lambda b, i: (rb(b, c_fwd(i)), 0))
    out_b = pl.BlockSpec((CHUNK, dr), lambda b, i: (rb(b, c_bwd(i)), 0))
    n_heads = dr // RWKV_DH
    return pl.pallas_call(
        functools.partial(_rwkv_kernel, nc=nc, nt=nc + nl),
        grid=(n_batch, nc + nl),
        in_specs=[spec(c_fwd, c_prev), spec(c_fwd, ident), spec(c_fwd, c_next),
                  spec(c_bwd, c_prev), spec(c_bwd, ident), spec(c_bwd, c_next),
                  _resident(conv.shape), _resident(vec.shape), _resident(w_up.shape),
                  _resident(a_up.shape), _resident(g_up.shape), _resident(bd.shape)],
        out_specs=[out_f, out_b, out_f, out_f],
        out_shape=[jax.ShapeDtypeStruct((rows, dr), F32)] * 4,
        scratch_shapes=[pltpu.VMEM((2, n_heads // 2, RWKV_DH, 2 * RWKV_DH), F32)],
        compiler_params=_cparams("arbitrary", "arbitrary"),
        name="rwkv7_chunk_scan",
    )(p_rw, p_rw, p_rw, p_rw, p_rw, p_rw, conv, vec, w_up, a_up, g_up, bd)


def _post_kernel(x_ref, m_ref, of_ref, ob_ref, go_ref, yf_ref, yb_ref, bonus_ref, gr_ref, vec_ref,
                 bd_ref, wo_ref, o_ref):
    vec = vec_ref[...]
    o_gain, ln_w, ln_b = vec[0:1], vec[1:2], vec[2:3]
    o = of_ref[...] + ob_ref[...]
    dh = o.shape[1] // HGRN_HEADS
    parts = []
    for h in range(HGRN_HEADS):
        oh = o[:, h * dh:(h + 1) * dh]
        parts.append(oh * lax.rsqrt(jnp.mean(oh * oh, axis=-1, keepdims=True) + RMS_EPS))
    o = jnp.concatenate(parts, axis=1) * o_gain * _silu(go_ref[...])
    y = yf_ref[...] + yb_ref[...]
    bd = bd_ref[...]
    inv = 1.0 / RWKV_DH
    mu = _dot_x01(y, bd) * inv
    yc = y - mu
    var = _dot_x01(yc * yc, bd) * inv
    y = yc * lax.rsqrt(var + GN_EPS) * ln_w + ln_b
    y = (y + bonus_ref[...]) * gr_ref[...]
    mixed = jnp.concatenate([o, y], axis=1).astype(BF16)
    out = jnp.dot(mixed, wo_ref[...], preferred_element_type=F32)
    o_ref[...] = x_ref[...] + m_ref[0][5:6] * out


def _post_call(xa, mods, o_f, o_b, p_hg, y_f, y_b, bonus, g_r, vec, bd, w_out, *, tm, ctx_tiles,
               tiles_per_batch, n_batch):
    rows, d = xa.shape
    dh = o_f.shape[1]
    dr = y_f.shape[1]
    row = lambda w: pl.BlockSpec((tm, w), lambda i: (i, 0))
    return pl.pallas_call(
        _post_kernel,
        grid=(rows // tm,),
        in_specs=[row(d),
                  pl.BlockSpec((1, N_MOD, d),
                               lambda i: (_mod_index(i, ctx_tiles, tiles_per_batch, n_batch), 0, 0)),
                  row(dh), row(dh), pl.BlockSpec((tm, dh), lambda i: (i, 4)),
                  row(dr), row(dr), row(dr), row(dr),
                  _resident(vec.shape), _resident(bd.shape), _resident(w_out.shape)],
        out_specs=row(d),
        out_shape=jax.ShapeDtypeStruct((rows, d), F32),
        compiler_params=_cparams("arbitrary"),
        name="mix_out_projection",
    )(xa, mods, o_f, o_b, p_hg, y_f, y_b, bonus, g_r, vec, bd, w_out)


def kernel(x, c, ctx, c_ctx, w_mod, b_mod, norm_gains, final_gain, ffn1_up, ffn1_down, ffn2_up, ffn2_down, w_in, w_out, hgrn_lb_logits, hgrn_o_gain, rwkv_conv, rwkv_w0, rwkv_w_up, rwkv_a0, rwkv_a_up, rwkv_g_up, rwkv_k_k, rwkv_k_a, rwkv_r_k, rwkv_ln_w, rwkv_ln_b):
    n_batch, seq, d = x.shape
    n_ctx = ctx.shape[1]
    depth = w_mod.shape[0]
    d_ff = ffn1_down.shape[1]
    d_hgrn = hgrn_o_gain.shape[1]
    d_rwkv = rwkv_k_k.shape[1]
    hg_cols = 5 * d_hgrn
    assert seq % CHUNK == 0 and n_ctx % CHUNK == 0 and GRID_W == CHUNK and n_batch < 8
    nc, nl = n_ctx // CHUNK, seq // CHUNK

    p = jax.nn.softmax(hgrn_lb_logits.astype(F32), axis=0)
    cum = jnp.cumsum(p, axis=0)
    lb = cum - cum[0]
    lbp = jnp.stack([jnp.log(lb), jnp.log1p(-lb), 1.0 - lb], axis=2)

    cvec = jnp.concatenate([c, c_ctx[None], jnp.zeros((7 - n_batch, d), F32)], axis=0)
    mods = _mod_call(cvec, w_mod, b_mod).reshape(depth, 8, N_MOD, d)

    bd = jnp.asarray(np.kron(np.eye(d_rwkv // RWKV_DH), np.ones((RWKV_DH, RWKV_DH))), BF16)

    xa = jnp.concatenate([ctx.reshape(n_batch * n_ctx, d), x.reshape(n_batch * seq, d)], axis=0)
    tm = _row_tile(n_batch * n_ctx, seq, 1024)
    tm_s = _row_tile(n_batch * n_ctx, seq, 512)
    tiles = dict(tm=tm, ctx_tiles=n_batch * n_ctx // tm, tiles_per_batch=seq // tm, n_batch=n_batch)
    tiles_s = dict(tm=tm_s, ctx_tiles=n_batch * n_ctx // tm_s, tiles_per_batch=seq // tm_s, n_batch=n_batch)

    for l in range(depth):
        last = l == depth - 1
        bf = lambda w: w.astype(BF16)
        xa = _ffn_call(xa, mods[l], norm_gains[l, 0], bf(ffn1_up[l, :, :d_ff]), bf(ffn1_up[l, :, d_ff:]),
                       bf(ffn1_down[l]), mod_base=0, **tiles)
        p_hg, p_rw = _inproj_call(xa, mods[l], norm_gains[l, 1], bf(w_in[l, :, :hg_cols]),
                                  bf(w_in[l, :, hg_cols:]), **tiles_s)
        o_f, o_b = _gla_call(p_hg, lbp[l], n_batch=n_batch, nc=nc, nl=nl)
        vec = jnp.concatenate([rwkv_w0[l], rwkv_a0[l], rwkv_k_k[l][None], rwkv_k_a[l][None],
                               rwkv_r_k[l].reshape(1, d_rwkv), jnp.zeros((1, d_rwkv), F32)], axis=0)
        y_f, y_b, bonus, g_r = _rwkv_call(p_rw, rwkv_conv[l].reshape(9, -1), vec, bf(rwkv_w_up[l]),
                                          bf(rwkv_a_up[l]), bf(rwkv_g_up[l]), bd,
                                          n_batch=n_batch, nc=nc, nl=nl)
        vec2 = jnp.concatenate([hgrn_o_gain[l][None], rwkv_ln_w[l][None], rwkv_ln_b[l][None],
                                jnp.zeros((5, d_rwkv), F32)], axis=0)
        xa = _post_call(xa, mods[l], o_f, o_b, p_hg, y_f, y_b, bonus, g_r, vec2, bd, bf(w_out[l]),
                        **tiles_s)
        up, down = bf(ffn2_up[l]), bf(ffn2_down[l])
        xa = _ffn_call(xa, mods[l], norm_gains[l, 2], up[:, :d_ff], up[:, d_ff:], down, mod_base=6,
                       skip_ctx=last, final_gain=final_gain if last else None, **tiles)
    return xa.reshape(n_batch, seq, d)
```

```python
import functools
import math

import numpy as np
import jax
import jax.numpy as jnp
from jax import lax
from jax.experimental import pallas as pl
from jax.experimental.pallas import tpu as pltpu

F32 = jnp.float32
BF16 = jnp.bfloat16

CHUNK = 64
GRID_W = 64
HGRN_HEADS = 4
RWKV_DH = 64
W_RANK = 64
A_RANK = 64
G_RANK = 128
N_MOD = 9
RMS_EPS = 1e-6
GN_EPS = 64e-5
VMEM_LIMIT = 56 * 1024 * 1024


def _cparams(*sem):
    return pltpu.CompilerParams(dimension_semantics=sem, vmem_limit_bytes=VMEM_LIMIT)


def _dot(a, b):
    return jnp.dot(a.astype(BF16), b.astype(BF16), preferred_element_type=F32)


def _dot_nt(a, b):
    return lax.dot_general(a.astype(BF16), b.astype(BF16), (((1,), (1,)), ((), ())),
                           preferred_element_type=F32)


def _dot_tn(a, b):
    return lax.dot_general(a.astype(BF16), b.astype(BF16), (((0,), (0,)), ((), ())),
                           preferred_element_type=F32)


def _split2(x):
    hi = x.astype(BF16)
    return hi, (x - hi.astype(F32)).astype(BF16)


def _dot01(m01, x):
    hi, lo = _split2(x)
    return (jnp.dot(m01, hi, preferred_element_type=F32)
            + jnp.dot(m01, lo, preferred_element_type=F32))


def _dot_x01(x, m01):
    hi, lo = _split2(x)
    return (jnp.dot(hi, m01, preferred_element_type=F32)
            + jnp.dot(lo, m01, preferred_element_type=F32))


def _sigmoid(x):
    return 0.5 * jnp.tanh(0.5 * x) + 0.5


def _silu(x):
    return x * _sigmoid(x)


def _rms_mod(x, gain, shift, scale):
    y = x * lax.rsqrt(jnp.mean(x * x, axis=-1, keepdims=True) + RMS_EPS)
    return (y * gain) * (1.0 + scale) + shift


def _mod_kernel(c_ref, w_ref, b_ref, o_ref):
    s = _silu(c_ref[...])
    o_ref[0] = jnp.dot(s, w_ref[0], preferred_element_type=F32,
                       precision=lax.Precision.HIGHEST) + b_ref[0]


def _mod_call(cvec, w_mod, b_mod):
    depth, d, n = w_mod.shape
    tn = 1152 if n % 1152 == 0 else n
    return pl.pallas_call(
        _mod_kernel,
        grid=(depth, n // tn),
        in_specs=[pl.BlockSpec((8, d), lambda l, j: (0, 0)),
                  pl.BlockSpec((1, d, tn), lambda l, j: (l, 0, j)),
                  pl.BlockSpec((1, 1, tn), lambda l, j: (l, 0, j))],
        out_specs=pl.BlockSpec((1, 8, tn), lambda l, j: (l, 0, j)),
        out_shape=jax.ShapeDtypeStruct((depth, 8, n), F32),
        compiler_params=_cparams("arbitrary", "arbitrary"),
        name="mod_vectors",
    )(cvec, w_mod, b_mod.reshape(depth, 1, n))


def _row_tile(n_ctx_rows, seq, cap):
    tm = cap
    while n_ctx_rows % tm or seq % tm:
        tm //= 2
    return tm


def _mod_index(i, ctx_tiles, tiles_per_batch, n_batch):
    return jnp.where(i < ctx_tiles, n_batch, (i - ctx_tiles) // tiles_per_batch)


def _resident(shape):
    return pl.BlockSpec(shape, lambda *_: (0,) * len(shape), pipeline_mode=pl.Buffered(1))


def _ffn_kernel(x_ref, m_ref, g_ref, wg_ref, wu_ref, wd_ref, *rest, mod_base, ff_chunk, final):
    if final:
        fg_ref, o_ref = rest
    else:
        (o_ref,) = rest
    x = x_ref[...]
    m = m_ref[0]
    h = _rms_mod(x, g_ref[...], m[mod_base:mod_base + 1], m[mod_base + 1:mod_base + 2]).astype(BF16)
    d_ff = wd_ref.shape[0]
    acc = jnp.zeros(x.shape, F32)
    for j in range(d_ff // ff_chunk):
        sl = slice(j * ff_chunk, (j + 1) * ff_chunk)
        gate = jnp.dot(h, wg_ref[:, sl], preferred_element_type=F32)
        up = jnp.dot(h, wu_ref[:, sl], preferred_element_type=F32)
        act = (_silu(gate) * up).astype(BF16)
        acc = acc + jnp.dot(act, wd_ref[sl, :], preferred_element_type=F32)
    y = x + (0.5 * m[mod_base + 2:mod_base + 3]) * acc
    if final:
        y = (y * lax.rsqrt(jnp.mean(y * y, axis=-1, keepdims=True) + RMS_EPS)) * fg_ref[...]
    o_ref[...] = y


def _ffn_call(xa, mods, gain, wg, wu, wd, *, mod_base, tm, ctx_tiles, tiles_per_batch, n_batch,
              skip_ctx=False, final_gain=None):
    rows, d = xa.shape
    d_ff = wd.shape[0]
    n_tiles = rows // tm
    t0 = ctx_tiles if skip_ctx else 0
    final = final_gain is not None
    in_specs = [pl.BlockSpec((tm, d), lambda i: (i + t0, 0)),
                pl.BlockSpec((1, N_MOD, d),
                             lambda i: (_mod_index(i + t0, ctx_tiles, tiles_per_batch, n_batch), 0, 0)),
                _resident((1, d)), _resident((d, d_ff)), _resident((d, d_ff)), _resident((d_ff, d))]
    args = [xa, mods, gain.reshape(1, d), wg, wu, wd]
    if final:
        in_specs.append(_resident((1, d)))
        args.append(final_gain.reshape(1, d))
    out_rows = rows - t0 * tm
    return pl.pallas_call(
        functools.partial(_ffn_kernel, mod_base=mod_base, ff_chunk=256, final=final),
        grid=(n_tiles - t0,),
        in_specs=in_specs,
        out_specs=pl.BlockSpec((tm, d), lambda i: (i, 0)),
        out_shape=jax.ShapeDtypeStruct((out_rows, d), F32),
        compiler_params=_cparams("arbitrary"),
        name="ffn_half_step",
    )(*args)


def _inproj_kernel(x_ref, m_ref, g_ref, whg_ref, wrw_ref, ohg_ref, orw_ref):
    m = m_ref[0]
    h = _rms_mod(x_ref[...], g_ref[...], m[3:4], m[4:5]).astype(BF16)
    ohg_ref[...] = jnp.dot(h, whg_ref[...], preferred_element_type=F32)
    orw_ref[...] = jnp.dot(h, wrw_ref[...], preferred_element_type=F32)


def _inproj_call(xa, mods, gain, w_hg, w_rw, *, tm, ctx_tiles, tiles_per_batch, n_batch):
    rows, d = xa.shape
    n_hg, n_rw = w_hg.shape[1], w_rw.shape[1]
    return pl.pallas_call(
        _inproj_kernel,
        grid=(rows // tm,),
        in_specs=[pl.BlockSpec((tm, d), lambda i: (i, 0)),
                  pl.BlockSpec((1, N_MOD, d),
                               lambda i: (_mod_index(i, ctx_tiles, tiles_per_batch, n_batch), 0, 0)),
                  _resident((1, d)), _resident((d, n_hg)), _resident((d, n_rw))],
        out_specs=[pl.BlockSpec((tm, n_hg), lambda i: (i, 0)),
                   pl.BlockSpec((tm, n_rw), lambda i: (i, 0))],
        out_shape=[jax.ShapeDtypeStruct((rows, n_hg), F32), jax.ShapeDtypeStruct((rows, n_rw), F32)],
        compiler_params=_cparams("arbitrary"),
        name="in_projection",
    )(xa, mods, gain.reshape(1, d), w_hg, w_rw)


def _chunk_maps(n_batch, nc, nl):
    nt = nc + nl

    def rb(b, c):
        return jnp.where(c < nc, b * nc + c, n_batch * nc + b * nl + (c - nc))

    def c_fwd(i):
        return i

    def c_bwd(i):
        return jnp.where(i < nc, nc - 1 - i, nt + nc - 1 - i)

    def c_prev(c):
        return jnp.where(c < nc, jnp.maximum(c - 1, 0), jnp.maximum(c - 1, nc))

    def c_next(c):
        return jnp.where(c < nc, jnp.minimum(c + 1, nc - 1), jnp.minimum(c + 1, nt - 1))

    return rb, c_fwd, c_bwd, c_prev, c_next


def _tri(rev, strict):
    r = lax.broadcasted_iota(jnp.int32, (CHUNK, CHUNK), 0)
    c = lax.broadcasted_iota(jnp.int32, (CHUNK, CHUNK), 1)
    if rev:
        return (r < c) if strict else (r <= c)
    return (r > c) if strict else (r >= c)


def _gla_tables():
    n = CHUNK
    idx = np.arange(n)
    incl = (idx[:, None] >= idx[None, :])
    rows = [incl, (idx[None, :] > idx[:, None])]
    masks = [np.eye(n, dtype=bool)]
    h = n // 2
    while h >= 1:
        blk = idx // (2 * h)
        mid = blk * 2 * h + h
        rowpart = (idx % (2 * h)) >= h
        as_row = (idx[None, :] > mid[:, None]) & (idx[None, :] <= idx[:, None]) & rowpart[:, None]
        as_col = (idx[None, :] > idx[:, None]) & (idx[None, :] <= mid[:, None]) & (~rowpart)[:, None]
        rows.append(as_row | as_col)
        masks.append((blk[:, None] == blk[None, :]) & rowpart[:, None] & (~rowpart)[None, :])
        h //= 2
    fwd = np.concatenate(rows, axis=0).astype(np.float32)
    mask_f = np.stack(masks).astype(np.float32)
    n_blocks = fwd.shape[0] // n
    rev = fwd.reshape(n_blocks, n, n)[:, ::-1, ::-1].reshape(fwd.shape)
    mask_r = mask_f[:, ::-1, ::-1]
    return np.stack([fwd, rev]), np.stack([mask_f, mask_r])


def _gla_chunks(dirs, lbp_ref, tab_ref, mask_ref, s_ref):
    c = CHUNK
    n_levels = mask_ref.shape[1] - 1
    bf = lambda t: t.astype(BF16)
    mm = lambda a, b: jnp.dot(a, b, preferred_element_type=F32)
    mm_nt = lambda a, b: lax.dot_general(a, b, (((1,), (1,)), ((), ())), preferred_element_type=F32)
    mm_tn = lambda a, b: lax.dot_general(a, b, (((0,), (0,)), ((), ())), preferred_element_type=F32)
    gates = []
    for d, (q_raw, z, v) in enumerate(dirs):
        lbp = lbp_ref[d]
        log_lb, log_1mlb, one_mlb = lbp[0:1], lbp[1:2], lbp[2:3]
        log_sig = jnp.minimum(z, 0.0) - jnp.log(1.0 + jnp.exp(-jnp.abs(z)))
        bv = log_1mlb + log_sig
        lf = jnp.maximum(log_lb, bv) + jnp.log(1.0 + jnp.exp(-jnp.abs(log_lb - bv)))
        gates.append((_silu(q_raw), one_mlb * _sigmoid(-z), v, lf))
    decays = [jnp.exp(_dot01(tab_ref[d], g[3])) for d, g in enumerate(gates)]
    dk = dirs[0][0].shape[1] // HGRN_HEADS
    chains = []
    for d, (q, key, v, _) in enumerate(gates):
        for h in range(HGRN_HEADS):
            sl = slice(h * dk, (h + 1) * dk)
            qh, kh, ex = q[:, sl], key[:, sl], decays[d][:, sl]
            chains.append(dict(
                d=d, h=h, q=bf(qh), k=bf(kh), v=bf(v[:, sl]),
                q_in=bf(qh * ex[0:c]), k_out=bf(kh * ex[c:2 * c]),
                w_tot=ex[0:1] if d == 1 else ex[c - 1:c],
                lv=[(bf(qh * ex[(2 + l) * c:(3 + l) * c]), bf(kh * ex[(2 + l) * c:(3 + l) * c]))
                    for l in range(n_levels)]))
    for ch in chains:
        masks = mask_ref[ch["d"]]
        att = masks[0] * mm_nt(ch["q"], ch["k"])
        for l, (ql, kl) in enumerate(ch["lv"]):
            att = att + masks[l + 1] * mm_nt(ql, kl)
        ch["att"] = bf(att)
    outs = [[None] * HGRN_HEADS for _ in dirs]
    for ch in chains:
        st = s_ref[ch["d"], ch["h"]]
        outs[ch["d"]][ch["h"]] = mm(ch["att"], ch["v"]) + mm_nt(ch["q_in"], bf(st))
        s_ref[ch["d"], ch["h"]] = st * ch["w_tot"] + mm_tn(ch["v"], ch["k_out"])
    return [jnp.concatenate(o, axis=1) for o in outs]


def _gla_kernel(qf_ref, zf_ref, vf_ref, qb_ref, zb_ref, vb_ref, lbp_ref, tab_ref, mask_ref,
                of_ref, ob_ref, s_ref):
    @pl.when(pl.program_id(1) == 0)
    def _():
        s_ref[...] = jnp.zeros(s_ref.shape, F32)

    o_f, o_b = _gla_chunks([(qf_ref[...], zf_ref[...], vf_ref[...]), (qb_ref[...], zb_ref[...], vb_ref[...])],
                           lbp_ref, tab_ref, mask_ref, s_ref)
    of_ref[...] = o_f
    ob_ref[...] = o_b


def _gla_call(p_hg, lbp, *, n_batch, nc, nl):
    rows = p_hg.shape[0]
    dh = p_hg.shape[1] // 5
    rb, c_fwd, c_bwd, _, _ = _chunk_maps(n_batch, nc, nl)
    tab, masks = _gla_tables()
    tab = jnp.asarray(tab, BF16)
    masks = jnp.asarray(masks, F32)

    def spec(col, cmap):
        return pl.BlockSpec((CHUNK, dh), lambda b, i: (rb(b, cmap(i)), col))

    out_f = pl.BlockSpec((CHUNK, dh), lambda b, i: (rb(b, c_fwd(i)), 0))
    out_b = pl.BlockSpec((CHUNK, dh), lambda b, i: (rb(b, c_bwd(i)), 0))
    dk = dh // HGRN_HEADS
    return pl.pallas_call(
        _gla_kernel,
        grid=(n_batch, nc + nl),
        in_specs=[spec(0, c_fwd), spec(1, c_fwd), spec(3, c_fwd),
                  spec(0, c_bwd), spec(2, c_bwd), spec(3, c_bwd),
                  _resident(lbp.shape), _resident(tab.shape), _resident(masks.shape)],
        out_specs=[out_f, out_b],
        out_shape=[jax.ShapeDtypeStruct((rows, dh), F32)] * 2,
        scratch_shapes=[pltpu.VMEM((2, HGRN_HEADS, dk, dk), F32)],
        compiler_params=_cparams("arbitrary", "arbitrary"),
        name="hgrn2_chunk_scan",
    )(p_hg, p_hg, p_hg, p_hg, p_hg, p_hg, lbp, tab, masks)


def _shift_conv(prev, cur, nxt, kern, is_lat, has_prev, has_next):
    n = cur.shape[0]
    row = lax.broadcasted_iota(jnp.int32, cur.shape, 0)
    lat = jnp.where(is_lat, 1.0, 0.0)
    hp = jnp.where(has_prev, 1.0, 0.0)
    hn = jnp.where(has_next, 1.0, 0.0)
    edge_l = (1.0 - lat) * hp
    edge_r = (1.0 - lat) * hn

    k_up = kern[0:3] * (lat * hp)
    k_dn = kern[6:9] * (lat * hn)

    def column(j):
        return prev * k_up[j:j + 1] + cur * kern[3 + j:4 + j] + nxt * k_dn[j:j + 1]

    fill_l = prev[n - 1:n] * (edge_l * kern[3:4])
    fill_r = nxt[0:1] * (edge_r * kern[5:6])
    return (column(1)
            + jnp.where(row == 0, fill_l, pltpu.roll(column(0), 1, 0))
            + jnp.where(row == n - 1, fill_r, pltpu.roll(column(2), n - 1, 0)))


def _pair_diag(x):
    lo = lax.broadcasted_iota(jnp.int32, x.shape, 1) < RWKV_DH
    z = jnp.zeros_like(x)
    return jnp.concatenate([jnp.where(lo, x, z), jnp.where(lo, z, x)], axis=0)


def _rwkv_scan_chunks(dirs, s_ref):
    c = CHUNK
    pw = 2 * RWKV_DH
    n_pairs = dirs[0][0].shape[1] // pw
    row = lax.broadcasted_iota(jnp.int32, (c, pw), 0)
    lane = lax.broadcasted_iota(jnp.int32, (c, pw), 1)
    col = lane & (RWKV_DH - 1)
    lo = lane < RWKV_DH
    eye = row == col
    bf = lambda t: t.astype(BF16)
    mm = lambda a, b: jnp.dot(a, b, preferred_element_type=F32)
    mm_nt = lambda a, b: lax.dot_general(a, b, (((1,), (1,)), ((), ())), preferred_element_type=F32)
    mm_tn = lambda a, b: lax.dot_general(a, b, (((0,), (0,)), ((), ())), preferred_element_type=F32)

    chains = []
    for d, (r, lw, k, v, kk, a) in enumerate(dirs):
        rev = d == 1
        tri = (row <= col) if rev else (row >= col)
        strict = (row < col) if rev else (row > col)
        b_incl = _dot01(jnp.where(_tri(rev, False), 1.0, 0.0).astype(BF16), lw)
        b_excl = b_incl - lw
        b_tot = b_incl[0:1] if rev else b_incl[c - 1:c]
        e_in = jnp.exp(-b_incl)
        e_out = jnp.exp(b_tot - b_incl)
        beta = kk * a
        a_t = -kk * jnp.exp(b_excl)
        r_t = r * jnp.exp(b_incl)
        b_t, k_t = beta * e_in, k * e_in
        b_h, k_h = beta * e_out, k * e_out
        w_tot = jnp.exp(b_tot)
        for j in range(n_pairs):
            sl = slice(j * pw, (j + 1) * pw)
            chains.append(dict(
                d=d, j=j, tri=tri, strict=strict, w_tot=w_tot[:, sl], v=v[:, sl],
                ar=bf(jnp.concatenate([a_t[:, sl], r_t[:, sl]], axis=0)),
                bt=_pair_diag(bf(b_t[:, sl])), kt=_pair_diag(bf(k_t[:, sl])),
                bk=bf(jnp.concatenate([b_h[:, sl], k_h[:, sl]], axis=0))))

    for ch in chains:
        g_b = mm_nt(ch["ar"], ch["bt"])
        g_k = mm_nt(ch["ar"], ch["kt"])
        a_ab = jnp.where(ch["strict"], g_b[0:c], 0.0)
        ch["a_rb"] = bf(jnp.where(ch["tri"], g_b[c:], 0.0))
        ch["a_k"] = bf(jnp.concatenate([jnp.where(ch["strict"], g_k[0:c], 0.0),
                                        jnp.where(ch["tri"], g_k[c:], 0.0)], axis=0))
        ch["p"] = jnp.where(eye, 1.0, a_ab)
        ch["ak"] = bf(a_ab)
    for ch in chains:
        ch["ak2"] = mm(ch["ak"], _pair_diag(ch["ak"]))
    n_lv = int(math.log2(c))
    for lv in range(2, n_lv + 1):
        for ch in chains:
            ak = bf(ch["ak2"])
            p_bd = _pair_diag(bf(ch["p"]))
            if lv < n_lv:
                both = mm(ak, jnp.concatenate([_pair_diag(ak), p_bd], axis=1))
                ch["ak2"] = both[:, :pw]
                ch["p"] = ch["p"] + both[:, pw:]
            else:
                ch["p"] = ch["p"] + mm(ak, p_bd)
    for ch in chains:
        s0 = s_ref[ch["d"], ch["j"]]
        ch["s0"] = s0
        ch["xs"] = mm_nt(ch["ar"], _pair_diag(bf(s0)))
        ch["av"] = mm(ch["a_k"], _pair_diag(bf(ch["v"])))
    for ch in chains:
        x = ch["xs"][0:c] + ch["av"][0:c]
        ch["u"] = mm(bf(ch["p"]), _pair_diag(bf(x)))
    outs = [[None] * n_pairs for _ in dirs]
    for ch in chains:
        u = ch["u"]
        outs[ch["d"]][ch["j"]] = ch["xs"][c:] + ch["av"][c:] + mm(ch["a_rb"], _pair_diag(bf(u)))
        z = mm_tn(bf(jnp.concatenate([u, ch["v"]], axis=0)), ch["bk"])
        s_ref[ch["d"], ch["j"]] = ch["s0"] * ch["w_tot"] + jnp.where(lo, z[0:c], z[c:])
    return [jnp.concatenate(o, axis=1) for o in outs]


def _rwkv_prep(rw, w0, w_up, a0, a_up, k_k, k_a, bd, d):
    dr = k_k.shape[1]
    r, k, v = rw[:, 0:dr], rw[:, dr:2 * dr], rw[:, 2 * dr:3 * dr]
    o = 3 * dr
    wd = rw[:, o + d * W_RANK:o + (d + 1) * W_RANK]
    o += 2 * W_RANK
    ad = rw[:, o + d * A_RANK:o + (d + 1) * A_RANK]
    x = w0 + _dot(jnp.tanh(wd), w_up)
    lw = -math.exp(-0.5) * _sigmoid(x)
    a = _sigmoid(a0 + _dot(ad, a_up))
    kk = k * k_k
    nrm = jnp.sqrt(_dot_x01(kk * kk, bd))
    kk = kk / jnp.maximum(nrm, 1e-12)
    k_dir = k * (1.0 + (a - 1.0) * k_a)
    return r, lw, k_dir, v, kk, a


def _rwkv_kernel(pf_ref, cf_ref, nf_ref, pb_ref, cb_ref, nb_ref, conv_ref, vec_ref, wup_ref, aup_ref,
                 gup_ref, bd_ref, yf_ref, yb_ref, bonus_ref, gr_ref, s_ref, *, nc, nt):
    i = pl.program_id(1)

    @pl.when(i == 0)
    def _():
        s_ref[...] = jnp.zeros(s_ref.shape, F32)

    kern = conv_ref[...]
    vec = vec_ref[...]
    bd = bd_ref[...]
    dr = bd.shape[0]
    w0, a0 = vec[0:2], vec[2:4]
    k_k, k_a, r_k = vec[4:5], vec[5:6], vec[6:7]

    def flags(c):
        is_lat = c >= nc
        has_prev = jnp.logical_and(c != 0, c != nc)
        has_next = jnp.logical_and(c != nc - 1, c != nt - 1)
        return is_lat, has_prev, has_next

    rw = _shift_conv(pf_ref[...], cf_ref[...], nf_ref[...], kern, *flags(i))
    fwd = _rwkv_prep(rw, w0[0:1], wup_ref[0], a0[0:1], aup_ref[0], k_k, k_a, bd, 0)
    r, _, _, v, _, a_f = fwd
    o = 3 * dr + 2 * W_RANK
    a_b = _sigmoid(a0[1:2] + _dot(rw[:, o + A_RANK:o + 2 * A_RANK], aup_ref[1]))
    k = rw[:, dr:2 * dr]
    k_sum = k * (2.0 + (a_f + a_b - 2.0) * k_a)
    bonus_ref[...] = _dot_x01(r * k_sum * r_k, bd) * v
    gd = rw[:, o + 2 * A_RANK:o + 2 * A_RANK + G_RANK]
    gr_ref[...] = _dot(_sigmoid(gd), gup_ref[...])

    cb = jnp.where(i < nc, nc - 1 - i, nt + nc - 1 - i)
    rw = _shift_conv(pb_ref[...], cb_ref[...], nb_ref[...], kern, *flags(cb))
    bwd = _rwkv_prep(rw, w0[1:2], wup_ref[1], a0[1:2], aup_ref[1], k_k, k_a, bd, 1)
    y_f, y_b = _rwkv_scan_chunks([fwd, bwd], s_ref)
    yf_ref[...] = y_f
    yb_ref[...] = y_b


def _rwkv_call(p_rw, conv, vec, w_up, a_up, g_up, bd, *, n_batch, nc, nl):
    rows, n_rw = p_rw.shape
    dr = bd.shape[0]
    rb, c_fwd, c_bwd, c_prev, c_next = _chunk_maps(n_batch, nc, nl)

    def spec(cmap, nb):
        return pl.BlockSpec((CHUNK, n_rw), lambda b, i: (rb(b, nb(cmap(i))), 0))

    ident = lambda c: c
    out_f = pl.BlockSpec((CHUNK, dr), lambda b, i: (rb(b, c_fwd(i)), 0))
    out_b = pl.BlockSpec((CHUNK, dr), lambda b, i: (rb(b, c_bwd(i)), 0))
    n_heads = dr // RWKV_DH
    return pl.pallas_call(
        functools.partial(_rwkv_kernel, nc=nc, nt=nc + nl),
        grid=(n_batch, nc + nl),
        in_specs=[spec(c_fwd, c_prev), spec(c_fwd, ident), spec(c_fwd, c_next),
                  spec(c_bwd, c_prev), spec(c_bwd, ident), spec(c_bwd, c_next),
                  _resident(conv.shape), _resident(vec.shape), _resident(w_up.shape),
                  _resident(a_up.shape), _resident(g_up.shape), _resident(bd.shape)],
        out_specs=[out_f, out_b, out_f, out_f],
        out_shape=[jax.ShapeDtypeStruct((rows, dr), F32)] * 4,
        scratch_shapes=[pltpu.VMEM((2, n_heads // 2, RWKV_DH, 2 * RWKV_DH), F32)],
        compiler_params=_cparams("arbitrary", "arbitrary"),
        name="rwkv7_chunk_scan",
    )(p_rw, p_rw, p_rw, p_rw, p_rw, p_rw, conv, vec, w_up, a_up, g_up, bd)


def _post_kernel(x_ref, m_ref, of_ref, ob_ref, go_ref, yf_ref, yb_ref, bonus_ref, gr_ref, vec_ref,
                 bd_ref, wo_ref, o_ref):
    vec = vec_ref[...]
    o_gain, ln_w, ln_b = vec[0:1], vec[1:2], vec[2:3]
    o = of_ref[...] + ob_ref[...]
    dh = o.shape[1] // HGRN_HEADS
    parts = []
    for h in range(HGRN_HEADS):
        oh = o[:, h * dh:(h + 1) * dh]
        parts.append(oh * lax.rsqrt(jnp.mean(oh * oh, axis=-1, keepdims=True) + RMS_EPS))
    o = jnp.concatenate(parts, axis=1) * o_gain * _silu(go_ref[...])
    y = yf_ref[...] + yb_ref[...]
    bd = bd_ref[...]
    inv = 1.0 / RWKV_DH
    mu = _dot_x01(y, bd) * inv
    yc = y - mu
    var = _dot_x01(yc * yc, bd) * inv
    y = yc * lax.rsqrt(var + GN_EPS) * ln_w + ln_b
    y = (y + bonus_ref[...]) * gr_ref[...]
    mixed = jnp.concatenate([o, y], axis=1).astype(BF16)
    out = jnp.dot(mixed, wo_ref[...], preferred_element_type=F32)
    o_ref[...] = x_ref[...] + m_ref[0][5:6] * out


def _post_call(xa, mods, o_f, o_b, p_hg, y_f, y_b, bonus, g_r, vec, bd, w_out, *, tm, ctx_tiles,
               tiles_per_batch, n_batch):
    rows, d = xa.shape
    dh = o_f.shape[1]
    dr = y_f.shape[1]
    row = lambda w: pl.BlockSpec((tm, w), lambda i: (i, 0))
    return pl.pallas_call(
        _post_kernel,
        grid=(rows // tm,),
        in_specs=[row(d),
                  pl.BlockSpec((1, N_MOD, d),
                               lambda i: (_mod_index(i, ctx_tiles, tiles_per_batch, n_batch), 0, 0)),
                  row(dh), row(dh), pl.BlockSpec((tm, dh), lambda i: (i, 4)),
                  row(dr), row(dr), row(dr), row(dr),
                  _resident(vec.shape), _resident(bd.shape), _resident(w_out.shape)],
        out_specs=row(d),
        out_shape=jax.ShapeDtypeStruct((rows, d), F32),
        compiler_params=_cparams("arbitrary"),
        name="mix_out_projection",
    )(xa, mods, o_f, o_b, p_hg, y_f, y_b, bonus, g_r, vec, bd, w_out)


def kernel(x, c, ctx, c_ctx, w_mod, b_mod, norm_gains, final_gain, ffn1_up, ffn1_down, ffn2_up, ffn2_down, w_in, w_out, hgrn_lb_logits, hgrn_o_gain, rwkv_conv, rwkv_w0, rwkv_w_up, rwkv_a0, rwkv_a_up, rwkv_g_up, rwkv_k_k, rwkv_k_a, rwkv_r_k, rwkv_ln_w, rwkv_ln_b):
    n_batch, seq, d = x.shape
    n_ctx = ctx.shape[1]
    depth = w_mod.shape[0]
    d_ff = ffn1_down.shape[1]
    d_hgrn = hgrn_o_gain.shape[1]
    d_rwkv = rwkv_k_k.shape[1]
    hg_cols = 5 * d_hgrn
    assert seq % CHUNK == 0 and n_ctx % CHUNK == 0 and GRID_W == CHUNK and n_batch < 8
    nc, nl = n_ctx // CHUNK, seq // CHUNK

    p = jax.nn.softmax(hgrn_lb_logits.astype(F32), axis=0)
    cum = jnp.cumsum(p, axis=0)
    lb = cum - cum[0]
    lbp = jnp.stack([jnp.log(lb), jnp.log1p(-lb), 1.0 - lb], axis=2)

    cvec = jnp.concatenate([c, c_ctx[None], jnp.zeros((7 - n_batch, d), F32)], axis=0)
    mods = _mod_call(cvec, w_mod, b_mod).reshape(depth, 8, N_MOD, d)

    bd = jnp.asarray(np.kron(np.eye(d_rwkv // RWKV_DH), np.ones((RWKV_DH, RWKV_DH))), BF16)

    xa = jnp.concatenate([ctx.reshape(n_batch * n_ctx, d), x.reshape(n_batch * seq, d)], axis=0)
    tm = _row_tile(n_batch * n_ctx, seq, 1024)
    tm_s = _row_tile(n_batch * n_ctx, seq, 512)
    tiles = dict(tm=tm, ctx_tiles=n_batch * n_ctx // tm, tiles_per_batch=seq // tm, n_batch=n_batch)
    tiles_s = dict(tm=tm_s, ctx_tiles=n_batch * n_ctx // tm_s, tiles_per_batch=seq // tm_s, n_batch=n_batch)

    for l in range(depth):
        last = l == depth - 1
        bf = lambda w: w.astype(BF16)
        xa = _ffn_call(xa, mods[l], norm_gains[l, 0], bf(ffn1_up[l, :, :d_ff]), bf(ffn1_up[l, :, d_ff:]),
                       bf(ffn1_down[l]), mod_base=0, **tiles)
        p_hg, p_rw = _inproj_call(xa, mods[l], norm_gains[l, 1], bf(w_in[l, :, :hg_cols]),
                                  bf(w_in[l, :, hg_cols:]), **tiles_s)
        o_f, o_b = _gla_call(p_hg, lbp[l], n_batch=n_batch, nc=nc, nl=nl)
        vec = jnp.concatenate([rwkv_w0[l], rwkv_a0[l], rwkv_k_k[l][None], rwkv_k_a[l][None],
                               rwkv_r_k[l].reshape(1, d_rwkv), jnp.zeros((1, d_rwkv), F32)], axis=0)
        y_f, y_b, bonus, g_r = _rwkv_call(p_rw, rwkv_conv[l].reshape(9, -1), vec, bf(rwkv_w_up[l]),
                                          bf(rwkv_a_up[l]), bf(rwkv_g_up[l]), bd,
                                          n_batch=n_batch, nc=nc, nl=nl)
        vec2 = jnp.concatenate([hgrn_o_gain[l][None], rwkv_ln_w[l][None], rwkv_ln_b[l][None],
                                jnp.zeros((5, d_rwkv), F32)], axis=0)
        xa = _post_call(xa, mods[l], o_f, o_b, p_hg, y_f, y_b, bonus, g_r, vec2, bd, bf(w_out[l]),
                        **tiles_s)
        up, down = bf(ffn2_up[l]), bf(ffn2_down[l])
        xa = _ffn_call(xa, mods[l], norm_gains[l, 2], up[:, :d_ff], up[:, d_ff:], down, mod_base=6,
                       skip_ctx=last, final_gain=final_gain if last else None, **tiles)
    return xa.reshape(n_batch, seq, d)
```

```python
import functools
import math

import numpy as np
import jax
import jax.numpy as jnp
from jax import lax
from jax.experimental import pallas as pl
from jax.experimental.pallas import tpu as pltpu

F32 = jnp.float32
BF16 = jnp.bfloat16

CHUNK = 64
GRID_W = 64
HGRN_HEADS = 4
RWKV_DH = 64
W_RANK = 64
A_RANK = 64
G_RANK = 128
N_MOD = 9
RMS_EPS = 1e-6
GN_EPS = 64e-5
VMEM_LIMIT = 56 * 1024 * 1024


def _cparams(*sem):
    return pltpu.CompilerParams(dimension_semantics=sem, vmem_limit_bytes=VMEM_LIMIT)


def _dot(a, b):
    return jnp.dot(a.astype(BF16), b.astype(BF16), preferred_element_type=F32)


def _dot_nt(a, b):
    return lax.dot_general(a.astype(BF16), b.astype(BF16), (((1,), (1,)), ((), ())),
                           preferred_element_type=F32)


def _dot_tn(a, b):
    return lax.dot_general(a.astype(BF16), b.astype(BF16), (((0,), (0,)), ((), ())),
                           preferred_element_type=F32)


def _split2(x):
    hi = x.astype(BF16)
    return hi, (x - hi.astype(F32)).astype(BF16)


def _dot01(m01, x):
    hi, lo = _split2(x)
    return (jnp.dot(m01, hi, preferred_element_type=F32)
            + jnp.dot(m01, lo, preferred_element_type=F32))


def _dot_x01(x, m01):
    hi, lo = _split2(x)
    return (jnp.dot(hi, m01, preferred_element_type=F32)
            + jnp.dot(lo, m01, preferred_element_type=F32))


def _sigmoid(x):
    return 0.5 * jnp.tanh(0.5 * x) + 0.5


def _silu(x):
    return x * _sigmoid(x)


def _rms_mod(x, gain, shift, scale):
    y = x * lax.rsqrt(jnp.mean(x * x, axis=-1, keepdims=True) + RMS_EPS)
    return (y * gain) * (1.0 + scale) + shift


def _mod_kernel(c_ref, w_ref, b_ref, o_ref):
    s = _silu(c_ref[...])
    o_ref[0] = jnp.dot(s, w_ref[0], preferred_element_type=F32,
                       precision=lax.Precision.HIGHEST) + b_ref[0]


def _mod_call(cvec, w_mod, b_mod):
    depth, d, n = w_mod.shape
    tn = 1152 if n % 1152 == 0 else n
    return pl.pallas_call(
        _mod_kernel,
        grid=(depth, n // tn),
        in_specs=[pl.BlockSpec((8, d), lambda l, j: (0, 0)),
                  pl.BlockSpec((1, d, tn), lambda l, j: (l, 0, j)),
                  pl.BlockSpec((1, 1, tn), lambda l, j: (l, 0, j))],
        out_specs=pl.BlockSpec((1, 8, tn), lambda l, j: (l, 0, j)),
        out_shape=jax.ShapeDtypeStruct((depth, 8, n), F32),
        compiler_params=_cparams("arbitrary", "arbitrary"),
        name="mod_vectors",
    )(cvec, w_mod, b_mod.reshape(depth, 1, n))


def _row_tile(n_ctx_rows, seq, cap):
    tm = cap
    while n_ctx_rows % tm or seq % tm:
        tm //= 2
    return tm


def _mod_index(i, ctx_tiles, tiles_per_batch, n_batch):
    return jnp.where(i < ctx_tiles, n_batch, (i - ctx_tiles) // tiles_per_batch)


def _resident(shape):
    return pl.BlockSpec(shape, lambda *_: (0,) * len(shape), pipeline_mode=pl.Buffered(1))


def _ffn_kernel(x_ref, m_ref, g_ref, wg_ref, wu_ref, wd_ref, *rest, mod_base, ff_chunk, final):
    if final:
        fg_ref, o_ref = rest
    else:
        (o_ref,) = rest
    x = x_ref[...]
    m = m_ref[0]
    h = _rms_mod(x, g_ref[...], m[mod_base:mod_base + 1], m[mod_base + 1:mod_base + 2]).astype(BF16)
    d_ff = wd_ref.shape[0]
    acc = jnp.zeros(x.shape, F32)
    for j in range(d_ff // ff_chunk):
        sl = slice(j * ff_chunk, (j + 1) * ff_chunk)
        gate = jnp.dot(h, wg_ref[:, sl], preferred_element_type=F32)
        up = jnp.dot(h, wu_ref[:, sl], preferred_element_type=F32)
        act = (_silu(gate) * up).astype(BF16)
        acc = acc + jnp.dot(act, wd_ref[sl, :], preferred_element_type=F32)
    y = x + (0.5 * m[mod_base + 2:mod_base + 3]) * acc
    if final:
        y = (y * lax.rsqrt(jnp.mean(y * y, axis=-1, keepdims=True) + RMS_EPS)) * fg_ref[...]
    o_ref[...] = y


def _ffn_call(xa, mods, gain, wg, wu, wd, *, mod_base, tm, ctx_tiles, tiles_per_batch, n_batch,
              skip_ctx=False, final_gain=None):
    rows, d = xa.shape
    d_ff = wd.shape[0]
    n_tiles = rows // tm
    t0 = ctx_tiles if skip_ctx else 0
    final = final_gain is not None
    in_specs = [pl.BlockSpec((tm, d), lambda i: (i + t0, 0)),
                pl.BlockSpec((1, N_MOD, d),
                             lambda i: (_mod_index(i + t0, ctx_tiles, tiles_per_batch, n_batch), 0, 0)),
                _resident((1, d)), _resident((d, d_ff)), _resident((d, d_ff)), _resident((d_ff, d))]
    args = [xa, mods, gain.reshape(1, d), wg, wu, wd]
    if final:
        in_specs.append(_resident((1, d)))
        args.append(final_gain.reshape(1, d))
    out_rows = rows - t0 * tm
    return pl.pallas_call(
        functools.partial(_ffn_kernel, mod_base=mod_base, ff_chunk=256, final=final),
        grid=(n_tiles - t0,),
        in_specs=in_specs,
        out_specs=pl.BlockSpec((tm, d), lambda i: (i, 0)),
        out_shape=jax.ShapeDtypeStruct((out_rows, d), F32),
        compiler_params=_cparams("arbitrary"),
        name="ffn_half_step",
    )(*args)


def _inproj_kernel(x_ref, m_ref, g_ref, whg_ref, wrw_ref, ohg_ref, orw_ref):
    m = m_ref[0]
    h = _rms_mod(x_ref[...], g_ref[...], m[3:4], m[4:5]).astype(BF16)
    ohg_ref[...] = jnp.dot(h, whg_ref[...], preferred_element_type=F32)
    orw_ref[...] = jnp.dot(h, wrw_ref[...], preferred_element_type=F32)


def _inproj_call(xa, mods, gain, w_hg, w_rw, *, tm, ctx_tiles, tiles_per_batch, n_batch):
    rows, d = xa.shape
    n_hg, n_rw = w_hg.shape[1], w_rw.shape[1]
    return pl.pallas_call(
        _inproj_kernel,
        grid=(rows // tm,),
        in_specs=[pl.BlockSpec((tm, d), lambda i: (i, 0)),
                  pl.BlockSpec((1, N_MOD, d),
                               lambda i: (_mod_index(i, ctx_tiles, tiles_per_batch, n_batch), 0, 0)),
                  _resident((1, d)), _resident((d, n_hg)), _resident((d, n_rw))],
        out_specs=[pl.BlockSpec((tm, n_hg), lambda i: (i, 0)),
                   pl.BlockSpec((tm, n_rw), lambda i: (i, 0))],
        out_shape=[jax.ShapeDtypeStruct((rows, n_hg), F32), jax.ShapeDtypeStruct((rows, n_rw), F32)],
        compiler_params=_cparams("arbitrary"),
        name="in_projection",
    )(xa, mods, gain.reshape(1, d), w_hg, w_rw)


def _chunk_maps(n_batch, nc, nl):
    nt = nc + nl

    def rb(b, c):
        return jnp.where(c < nc, b * nc + c, n_batch * nc + b * nl + (c - nc))

    def c_fwd(i):
        return i

    def c_bwd(i):
        return jnp.where(i < nc, nc - 1 - i, nt + nc - 1 - i)

    def c_prev(c):
        return jnp.where(c < nc, jnp.maximum(c - 1, 0), jnp.maximum(c - 1, nc))

    def c_next(c):
        return jnp.where(c < nc, jnp.minimum(c + 1, nc - 1), jnp.minimum(c + 1, nt - 1))

    return rb, c_fwd, c_bwd, c_prev, c_next


def _interleave(*stages):
    values = [None] * len(stages)
    live = list(range(len(stages)))
    while live:
        for g in tuple(live):
            try:
                next(stages[g])
            except StopIteration as stop:
                values[g] = stop.value
                live.remove(g)
    return values


def _tri(rev, strict):
    r = lax.broadcasted_iota(jnp.int32, (CHUNK, CHUNK), 0)
    c = lax.broadcasted_iota(jnp.int32, (CHUNK, CHUNK), 1)
    if rev:
        return (r < c) if strict else (r <= c)
    return (r > c) if strict else (r >= c)


def _gla_tables():
    n = CHUNK
    idx = np.arange(n)
    incl = (idx[:, None] >= idx[None, :])
    rows = [incl, (idx[None, :] > idx[:, None])]
    masks = [np.eye(n, dtype=bool)]
    h = n // 2
    while h >= 1:
        blk = idx // (2 * h)
        mid = blk * 2 * h + h
        rowpart = (idx % (2 * h)) >= h
        as_row = (idx[None, :] > mid[:, None]) & (idx[None, :] <= idx[:, None]) & rowpart[:, None]
        as_col = (idx[None, :] > idx[:, None]) & (idx[None, :] <= mid[:, None]) & (~rowpart)[:, None]
        rows.append(as_row | as_col)
        masks.append((blk[:, None] == blk[None, :]) & rowpart[:, None] & (~rowpart)[None, :])
        h //= 2
    fwd = np.concatenate(rows, axis=0).astype(np.float32)
    mask_f = np.stack(masks).astype(np.float32)
    n_blocks = fwd.shape[0] // n
    rev = fwd.reshape(n_blocks, n, n)[:, ::-1, ::-1].reshape(fwd.shape)
    mask_r = mask_f[:, ::-1, ::-1]
    return np.stack([fwd, rev]), np.stack([mask_f, mask_r])


def _gla_gates(inputs, lbp_ref, tab_ref, g_ref, e_ref):
    dk = g_ref.shape[3] // HGRN_HEADS
    for d, (q_ref, z_ref, v_ref) in enumerate(inputs):
        for h in range(HGRN_HEADS):
            sl = slice(h * dk, (h + 1) * dk)
            lbp = lbp_ref[d, :, sl]
            log_lb, log_1mlb, one_mlb = lbp[0:1], lbp[1:2], lbp[2:3]
            z = z_ref[:, sl]
            log_sig = jnp.minimum(z, 0.0) - jnp.log(1.0 + jnp.exp(-jnp.abs(z)))
            bv = log_1mlb + log_sig
            lf = jnp.maximum(log_lb, bv) + jnp.log(1.0 + jnp.exp(-jnp.abs(log_lb - bv)))
            g_ref[d, 0, :, sl] = _silu(q_ref[:, sl])
            g_ref[d, 1, :, sl] = one_mlb * _sigmoid(-z)
            g_ref[d, 2, :, sl] = v_ref[:, sl]
            e_ref[d, :, sl] = jnp.exp(_dot01(tab_ref[d], lf))
            yield


def _gla_scan(g_ref, e_ref, mask_ref, s_ref):
    c = CHUNK
    n_levels = mask_ref.shape[1] - 1
    bf = lambda t: t.astype(BF16)
    mm = lambda a, b: jnp.dot(a, b, preferred_element_type=F32)
    mm_nt = lambda a, b: lax.dot_general(a, b, (((1,), (1,)), ((), ())), preferred_element_type=F32)
    mm_tn = lambda a, b: lax.dot_general(a, b, (((0,), (0,)), ((), ())), preferred_element_type=F32)
    dk = g_ref.shape[3] // HGRN_HEADS
    chains = []
    for d in range(2):
        masks = mask_ref[d]
        for h in range(HGRN_HEADS):
            sl = slice(h * dk, (h + 1) * dk)
            qh, kh, ex = g_ref[d, 0, :, sl], g_ref[d, 1, :, sl], e_ref[d, :, sl]
            ch = dict(d=d, h=h, v=bf(g_ref[d, 2, :, sl]),
                      q_in=bf(qh * ex[0:c]), k_out=bf(kh * ex[c:2 * c]),
                      w_tot=ex[0:1] if d == 1 else ex[c - 1:c])
            att = masks[0] * mm_nt(bf(qh), bf(kh))
            for l in range(n_levels):
                lv = ex[(2 + l) * c:(3 + l) * c]
                att = att + masks[l + 1] * mm_nt(bf(qh * lv), bf(kh * lv))
            ch["att"] = bf(att)
            chains.append(ch)
            yield
    outs = [[None] * HGRN_HEADS for _ in range(2)]
    for ch in chains:
        st = s_ref[ch["d"], ch["h"]]
        outs[ch["d"]][ch["h"]] = mm(ch["att"], ch["v"]) + mm_nt(ch["q_in"], bf(st))
        s_ref[ch["d"], ch["h"]] = st * ch["w_tot"] + mm_tn(ch["v"], ch["k_out"])
    return [jnp.concatenate(o, axis=1) for o in outs]


def _gla_kernel(qf_ref, zf_ref, vf_ref, qb_ref, zb_ref, vb_ref, lbp_ref, tab_ref, mask_ref,
                of_ref, ob_ref, s_ref, g_ref, e_ref):
    i = pl.program_id(1)

    @pl.when(i == 0)
    def _():
        g_ref[...] = jnp.zeros(g_ref.shape, F32)
        e_ref[...] = jnp.zeros(e_ref.shape, F32)

    @pl.when(i <= 1)
    def _():
        s_ref[...] = jnp.zeros(s_ref.shape, F32)

    (o_f, o_b), _ = _interleave(
        _gla_scan(g_ref, e_ref, mask_ref, s_ref),
        _gla_gates(((qf_ref, zf_ref, vf_ref), (qb_ref, zb_ref, vb_ref)), lbp_ref, tab_ref, g_ref, e_ref))
    of_ref[...] = o_f
    ob_ref[...] = o_b


def _gla_call(p_hg, lbp, *, n_batch, nc, nl):
    rows = p_hg.shape[0]
    dh = p_hg.shape[1] // 5
    nt = nc + nl
    rb, c_fwd, c_bwd, _, _ = _chunk_maps(n_batch, nc, nl)
    tab, masks = _gla_tables()
    tab = jnp.asarray(tab, BF16)
    masks = jnp.asarray(masks, F32)
    nxt = lambda i: jnp.minimum(i, nt - 1)
    cur = lambda i: jnp.maximum(i - 1, 0)

    def spec(col, cmap):
        return pl.BlockSpec((CHUNK, dh), lambda b, i: (rb(b, cmap(nxt(i))), col))

    out_f = pl.BlockSpec((CHUNK, dh), lambda b, i: (rb(b, c_fwd(cur(i))), 0))
    out_b = pl.BlockSpec((CHUNK, dh), lambda b, i: (rb(b, c_bwd(cur(i))), 0))
    dk = dh // HGRN_HEADS
    return pl.pallas_call(
        _gla_kernel,
        grid=(n_batch, nt + 1),
        in_specs=[spec(0, c_fwd), spec(1, c_fwd), spec(3, c_fwd),
                  spec(0, c_bwd), spec(2, c_bwd), spec(3, c_bwd),
                  _resident(lbp.shape), _resident(tab.shape), _resident(masks.shape)],
        out_specs=[out_f, out_b],
        out_shape=[jax.ShapeDtypeStruct((rows, dh), F32)] * 2,
        scratch_shapes=[pltpu.VMEM((2, HGRN_HEADS, dk, dk), F32),
                        pltpu.VMEM((2, 3, CHUNK, dh), F32),
                        pltpu.VMEM((2, tab.shape[1], dh), F32)],
        compiler_params=_cparams("arbitrary", "arbitrary"),
        name="hgrn2_chunk_scan",
    )(p_hg, p_hg, p_hg, p_hg, p_hg, p_hg, lbp, tab, masks)


def _shift_conv(prev, cur, nxt, kern, is_lat, has_prev, has_next):
    n = cur.shape[0]
    row = lax.broadcasted_iota(jnp.int32, cur.shape, 0)
    lat = jnp.where(is_lat, 1.0, 0.0)
    hp = jnp.where(has_prev, 1.0, 0.0)
    hn = jnp.where(has_next, 1.0, 0.0)
    edge_l = (1.0 - lat) * hp
    edge_r = (1.0 - lat) * hn

    k_up = kern[0:3] * (lat * hp)
    k_dn = kern[6:9] * (lat * hn)

    def column(j):
        return prev * k_up[j:j + 1] + cur * kern[3 + j:4 + j] + nxt * k_dn[j:j + 1]

    fill_l = prev[n - 1:n] * (edge_l * kern[3:4])
    fill_r = nxt[0:1] * (edge_r * kern[5:6])
    out = column(1)
    yield
    out = out + jnp.where(row == 0, fill_l, pltpu.roll(column(0), 1, 0))
    yield
    return out + jnp.where(row == n - 1, fill_r, pltpu.roll(column(2), n - 1, 0))


def _pair_diag(x):
    lo = lax.broadcasted_iota(jnp.int32, x.shape, 1) < RWKV_DH
    z = jnp.zeros_like(x)
    return jnp.concatenate([jnp.where(lo, x, z), jnp.where(lo, z, x)], axis=0)


def _rwkv_scan_chunks(dirs, s_ref):
    c = CHUNK
    pw = 2 * RWKV_DH
    n_pairs = dirs[0][0].shape[1] // pw
    row = lax.broadcasted_iota(jnp.int32, (c, pw), 0)
    lane = lax.broadcasted_iota(jnp.int32, (c, pw), 1)
    col = lane & (RWKV_DH - 1)
    lo = lane < RWKV_DH
    eye = row == col
    bf = lambda t: t.astype(BF16)
    mm = lambda a, b: jnp.dot(a, b, preferred_element_type=F32)
    mm_nt = lambda a, b: lax.dot_general(a, b, (((1,), (1,)), ((), ())), preferred_element_type=F32)
    mm_tn = lambda a, b: lax.dot_general(a, b, (((0,), (0,)), ((), ())), preferred_element_type=F32)

    chains = []
    for d, (r, lw, k, v, kk, a) in enumerate(dirs):
        rev = d == 1
        tri = (row <= col) if rev else (row >= col)
        strict = (row < col) if rev else (row > col)
        b_incl = _dot01(jnp.where(_tri(rev, False), 1.0, 0.0).astype(BF16), lw)
        b_excl = b_incl - lw
        b_tot = b_incl[0:1] if rev else b_incl[c - 1:c]
        e_in = jnp.exp(-b_incl)
        e_out = jnp.exp(b_tot - b_incl)
        beta = kk * a
        a_t = -kk * jnp.exp(b_excl)
        r_t = r * jnp.exp(b_incl)
        b_t, k_t = beta * e_in, k * e_in
        b_h, k_h = beta * e_out, k * e_out
        w_tot = jnp.exp(b_tot)
        for j in range(n_pairs):
            sl = slice(j * pw, (j + 1) * pw)
            chains.append(dict(
                d=d, j=j, tri=tri, strict=strict, w_tot=w_tot[:, sl], v=v[:, sl],
                ar=bf(jnp.concatenate([a_t[:, sl], r_t[:, sl]], axis=0)),
                bt=_pair_diag(bf(b_t[:, sl])), kt=_pair_diag(bf(k_t[:, sl])),
                bk=bf(jnp.concatenate([b_h[:, sl], k_h[:, sl]], axis=0))))
    yield
    for ch in chains:
        g_b = mm_nt(ch["ar"], ch["bt"])
        g_k = mm_nt(ch["ar"], ch["kt"])
        a_ab = jnp.where(ch["strict"], g_b[0:c], 0.0)
        ch["a_rb"] = bf(jnp.where(ch["tri"], g_b[c:], 0.0))
        ch["a_k"] = bf(jnp.concatenate([jnp.where(ch["strict"], g_k[0:c], 0.0),
                                        jnp.where(ch["tri"], g_k[c:], 0.0)], axis=0))
        ch["p"] = jnp.where(eye, 1.0, a_ab)
        ch["ak"] = bf(a_ab)
    yield
    for ch in chains:
        ch["ak2"] = mm(ch["ak"], _pair_diag(ch["ak"]))
    yield
    n_lv = int(math.log2(c))
    for lv in range(2, n_lv + 1):
        for ch in chains:
            ak = bf(ch["ak2"])
            p_bd = _pair_diag(bf(ch["p"]))
            if lv < n_lv:
                both = mm(ak, jnp.concatenate([_pair_diag(ak), p_bd], axis=1))
                ch["ak2"] = both[:, :pw]
                ch["p"] = ch["p"] + both[:, pw:]
            else:
                ch["p"] = ch["p"] + mm(ak, p_bd)
        yield
    for ch in chains:
        s0 = s_ref[ch["d"], ch["j"]]
        ch["s0"] = s0
        ch["xs"] = mm_nt(ch["ar"], _pair_diag(bf(s0)))
        ch["av"] = mm(ch["a_k"], _pair_diag(bf(ch["v"])))
    yield
    for ch in chains:
        x = ch["xs"][0:c] + ch["av"][0:c]
        ch["u"] = mm(bf(ch["p"]), _pair_diag(bf(x)))
    yield
    outs = [[None] * n_pairs for _ in dirs]
    for ch in chains:
        u = ch["u"]
        outs[ch["d"]][ch["j"]] = ch["xs"][c:] + ch["av"][c:] + mm(ch["a_rb"], _pair_diag(bf(u)))
        z = mm_tn(bf(jnp.concatenate([u, ch["v"]], axis=0)), ch["bk"])
        s_ref[ch["d"], ch["j"]] = ch["s0"] * ch["w_tot"] + jnp.where(lo, z[0:c], z[c:])
    return [jnp.concatenate(o, axis=1) for o in outs]


def _rwkv_prep(rw, w0, w_up, a0, a_up, k_k, k_a, bd, d):
    dr = k_k.shape[1]
    r, k, v = rw[:, 0:dr], rw[:, dr:2 * dr], rw[:, 2 * dr:3 * dr]
    o = 3 * dr
    wd = rw[:, o + d * W_RANK:o + (d + 1) * W_RANK]
    o += 2 * W_RANK
    ad = rw[:, o + d * A_RANK:o + (d + 1) * A_RANK]
    x = w0 + _dot(jnp.tanh(wd), w_up)
    lw = -math.exp(-0.5) * _sigmoid(x)
    a = _sigmoid(a0 + _dot(ad, a_up))
    kk = k * k_k
    nrm = jnp.sqrt(_dot_x01(kk * kk, bd))
    kk = kk / jnp.maximum(nrm, 1e-12)
    k_dir = k * (1.0 + (a - 1.0) * k_a)
    return r, lw, k_dir, v, kk, a


def _rwkv_kernel(pf_ref, cf_ref, nf_ref, pb_ref, cb_ref, nb_ref, conv_ref, vec_ref, wup_ref, aup_ref,
                 gup_ref, bd_ref, yf_ref, yb_ref, bonus_ref, gr_ref, s_ref, q_ref, *, nc, nt):
    step = pl.program_id(1)

    @pl.when(step == 0)
    def _():
        q_ref[...] = jnp.zeros(q_ref.shape, F32)

    @pl.when(step <= 1)
    def _():
        s_ref[...] = jnp.zeros(s_ref.shape, F32)

    scan = _rwkv_scan_chunks([tuple(q_ref[d, j] for j in range(6)) for d in range(2)], s_ref)
    next(scan)

    i = jnp.minimum(step, nt - 1)
    kern = conv_ref[...]
    vec = vec_ref[...]
    bd = bd_ref[...]
    dr = bd.shape[0]
    w0, a0 = vec[0:2], vec[2:4]
    k_k, k_a, r_k = vec[4:5], vec[5:6], vec[6:7]

    def flags(c):
        is_lat = c >= nc
        has_prev = jnp.logical_and(c != 0, c != nc)
        has_next = jnp.logical_and(c != nc - 1, c != nt - 1)
        return is_lat, has_prev, has_next

    def prepare():
        rw = yield from _shift_conv(pf_ref[...], cf_ref[...], nf_ref[...], kern, *flags(i))
        fwd = _rwkv_prep(rw, w0[0:1], wup_ref[0], a0[0:1], aup_ref[0], k_k, k_a, bd, 0)
        r, _, _, v, _, a_f = fwd
        yield
        o = 3 * dr + 2 * W_RANK
        a_b = _sigmoid(a0[1:2] + _dot(rw[:, o + A_RANK:o + 2 * A_RANK], aup_ref[1]))
        k = rw[:, dr:2 * dr]
        k_sum = k * (2.0 + (a_f + a_b - 2.0) * k_a)
        bonus_ref[...] = _dot_x01(r * k_sum * r_k, bd) * v
        gd = rw[:, o + 2 * A_RANK:o + 2 * A_RANK + G_RANK]
        gr_ref[...] = _dot(_sigmoid(gd), gup_ref[...])
        yield
        cb = jnp.where(i < nc, nc - 1 - i, nt + nc - 1 - i)
        rw = yield from _shift_conv(pb_ref[...], cb_ref[...], nb_ref[...], kern, *flags(cb))
        bwd = _rwkv_prep(rw, w0[1:2], wup_ref[1], a0[1:2], aup_ref[1], k_k, k_a, bd, 1)
        return fwd, bwd

    (y_f, y_b), prepared = _interleave(scan, prepare())
    yf_ref[...] = y_f
    yb_ref[...] = y_b
    for d, tensors in enumerate(prepared):
        for j, t in enumerate(tensors):
            q_ref[d, j] = t


def _rwkv_call(p_rw, conv, vec, w_up, a_up, g_up, bd, *, n_batch, nc, nl):
    rows, n_rw = p_rw.shape
    dr = bd.shape[0]
    nt = nc + nl
    rb, c_fwd, c_bwd, c_prev, c_next = _chunk_maps(n_batch, nc, nl)
    nxt = lambda i: jnp.minimum(i, nt - 1)
    cur = lambda i: jnp.maximum(i - 1, 0)

    def spec(cmap, nb):
        return pl.BlockSpec((CHUNK, n_rw), lambda b, i: (rb(b, nb(cmap(nxt(i)))), 0))

    ident = lambda c: c
    out_f = pl.BlockSpec((CHUNK, dr), lambda b, i: (rb(b, c_fwd(cur(i))), 0))
    out_b = pl.BlockSpec((CHUNK, dr), lambda b, i: (rb(b, c_bwd(cur(i))), 0))
    out_x = pl.BlockSpec((CHUNK, dr), lambda b, i: (rb(b, c_fwd(nxt(i))), 0))
    n_heads = dr // RWKV_DH
    return pl.pallas_call(
        functools.partial(_rwkv_kernel, nc=nc, nt=nt),
        grid=(n_batch, nt + 1),
        in_specs=[spec(c_fwd, c_prev), spec(c_fwd, ident), spec(c_fwd, c_next),
                  spec(c_bwd, c_prev), spec(c_bwd, ident), spec(c_bwd, c_next),
                  _resident(conv.shape), _resident(vec.shape), _resident(w_up.shape),
                  _resident(a_up.shape), _resident(g_up.shape), _resident(bd.shape)],
        out_specs=[out_f, out_b, out_x, out_x],
        out_shape=[jax.ShapeDtypeStruct((rows, dr), F32)] * 4,
        scratch_shapes=[pltpu.VMEM((2, n_heads // 2, RWKV_DH, 2 * RWKV_DH), F32),
                        pltpu.VMEM((2, 6, CHUNK, dr), F32)],
        compiler_params=_cparams("arbitrary", "arbitrary"),
        name="rwkv7_chunk_scan",
    )(p_rw, p_rw, p_rw, p_rw, p_rw, p_rw, conv, vec, w_up, a_up, g_up, bd)


def _post_kernel(x_ref, m_ref, of_ref, ob_ref, go_ref, yf_ref, yb_ref, bonus_ref, gr_ref, vec_ref,
                 bd_ref, wo_ref, o_ref):
    vec = vec_ref[...]
    o_gain, ln_w, ln_b = vec[0:1], vec[1:2], vec[2:3]
    o = of_ref[...] + ob_ref[...]
    dh = o.shape[1] // HGRN_HEADS
    parts = []
    for h in range(HGRN_HEADS):
        oh = o[:, h * dh:(h + 1) * dh]
        parts.append(oh * lax.rsqrt(jnp.mean(oh * oh, axis=-1, keepdims=True) + RMS_EPS))
    o = jnp.concatenate(parts, axis=1) * o_gain * _silu(go_ref[...])
    y = yf_ref[...] + yb_ref[...]
    bd = bd_ref[...]
    inv = 1.0 / RWKV_DH
    mu = _dot_x01(y, bd) * inv
    yc = y - mu
    var = _dot_x01(yc * yc, bd) * inv
    y = yc * lax.rsqrt(var + GN_EPS) * ln_w + ln_b
    y = (y + bonus_ref[...]) * gr_ref[...]
    mixed = jnp.concatenate([o, y], axis=1).astype(BF16)
    out = jnp.dot(mixed, wo_ref[...], preferred_element_type=F32)
    o_ref[...] = x_ref[...] + m_ref[0][5:6] * out


def _post_call(xa, mods, o_f, o_b, p_hg, y_f, y_b, bonus, g_r, vec, bd, w_out, *, tm, ctx_tiles,
               tiles_per_batch, n_batch):
    rows, d = xa.shape
    dh = o_f.shape[1]
    dr = y_f.shape[1]
    row = lambda w: pl.BlockSpec((tm, w), lambda i: (i, 0))
    return pl.pallas_call(
        _post_kernel,
        grid=(rows // tm,),
        in_specs=[row(d),
                  pl.BlockSpec((1, N_MOD, d),
                               lambda i: (_mod_index(i, ctx_tiles, tiles_per_batch, n_batch), 0, 0)),
                  row(dh), row(dh), pl.BlockSpec((tm, dh), lambda i: (i, 4)),
                  row(dr), row(dr), row(dr), row(dr),
                  _resident(vec.shape), _resident(bd.shape), _resident(w_out.shape)],
        out_specs=row(d),
        out_shape=jax.ShapeDtypeStruct((rows, d), F32),
        compiler_params=_cparams("arbitrary"),
        name="mix_out_projection",
    )(xa, mods, o_f, o_b, p_hg, y_f, y_b, bonus, g_r, vec, bd, w_out)


def kernel(x, c, ctx, c_ctx, w_mod, b_mod, norm_gains, final_gain, ffn1_up, ffn1_down, ffn2_up, ffn2_down, w_in, w_out, hgrn_lb_logits, hgrn_o_gain, rwkv_conv, rwkv_w0, rwkv_w_up, rwkv_a0, rwkv_a_up, rwkv_g_up, rwkv_k_k, rwkv_k_a, rwkv_r_k, rwkv_ln_w, rwkv_ln_b):
    n_batch, seq, d = x.shape
    n_ctx = ctx.shape[1]
    depth = w_mod.shape[0]
    d_ff = ffn1_down.shape[1]
    d_hgrn = hgrn_o_gain.shape[1]
    d_rwkv = rwkv_k_k.shape[1]
    hg_cols = 5 * d_hgrn
    assert seq % CHUNK == 0 and n_ctx % CHUNK == 0 and GRID_W == CHUNK and n_batch < 8
    nc, nl = n_ctx // CHUNK, seq // CHUNK

    p = jax.nn.softmax(hgrn_lb_logits.astype(F32), axis=0)
    cum = jnp.cumsum(p, axis=0)
    lb = cum - cum[0]
    lbp = jnp.stack([jnp.log(lb), jnp.log1p(-lb), 1.0 - lb], axis=2)

    cvec = jnp.concatenate([c, c_ctx[None], jnp.zeros((7 - n_batch, d), F32)], axis=0)
    mods = _mod_call(cvec, w_mod, b_mod).reshape(depth, 8, N_MOD, d)

    bd = jnp.asarray(np.kron(np.eye(d_rwkv // RWKV_DH), np.ones((RWKV_DH, RWKV_DH))), BF16)

    xa = jnp.concatenate([ctx.reshape(n_batch * n_ctx, d), x.reshape(n_batch * seq, d)], axis=0)
    tm = _row_tile(n_batch * n_ctx, seq, 1024)
    tm_s = _row_tile(n_batch * n_ctx, seq, 512)
    tiles = dict(tm=tm, ctx_tiles=n_batch * n_ctx // tm, tiles_per_batch=seq // tm, n_batch=n_batch)
    tiles_s = dict(tm=tm_s, ctx_tiles=n_batch * n_ctx // tm_s, tiles_per_batch=seq // tm_s, n_batch=n_batch)

    for l in range(depth):
        last = l == depth - 1
        bf = lambda w: w.astype(BF16)
        xa = _ffn_call(xa, mods[l], norm_gains[l, 0], bf(ffn1_up[l, :, :d_ff]), bf(ffn1_up[l, :, d_ff:]),
                       bf(ffn1_down[l]), mod_base=0, **tiles)
        p_hg, p_rw = _inproj_call(xa, mods[l], norm_gains[l, 1], bf(w_in[l, :, :hg_cols]),
                                  bf(w_in[l, :, hg_cols:]), **tiles_s)
        o_f, o_b = _gla_call(p_hg, lbp[l], n_batch=n_batch, nc=nc, nl=nl)
        vec = jnp.concatenate([rwkv_w0[l], rwkv_a0[l], rwkv_k_k[l][None], rwkv_k_a[l][None],
                               rwkv_r_k[l].reshape(1, d_rwkv), jnp.zeros((1, d_rwkv), F32)], axis=0)
        y_f, y_b, bonus, g_r = _rwkv_call(p_rw, rwkv_conv[l].reshape(9, -1), vec, bf(rwkv_w_up[l]),
                                          bf(rwkv_a_up[l]), bf(rwkv_g_up[l]), bd,
                                          n_batch=n_batch, nc=nc, nl=nl)
        vec2 = jnp.concatenate([hgrn_o_gain[l][None], rwkv_ln_w[l][None], rwkv_ln_b[l][None],
                                jnp.zeros((5, d_rwkv), F32)], axis=0)
        xa = _post_call(xa, mods[l], o_f, o_b, p_hg, y_f, y_b, bonus, g_r, vec2, bd, bf(w_out[l]),
                        **tiles_s)
        up, down = bf(ffn2_up[l]), bf(ffn2_down[l])
        xa = _ffn_call(xa, mods[l], norm_gains[l, 2], up[:, :d_ff], up[:, d_ff:], down, mod_base=6,
                       skip_ctx=last, final_gain=final_gain if last else None, **tiles)
    return xa.reshape(n_batch, seq, d)
```

```python
import functools
import math

import numpy as np
import jax
import jax.numpy as jnp
from jax import lax
from jax.experimental import pallas as pl
from jax.experimental.pallas import tpu as pltpu

F32 = jnp.float32
BF16 = jnp.bfloat16

CHUNK = 64
GRID_W = 64
HGRN_HEADS = 4
RWKV_DH = 64
LANES = 128
W_RANK = 64
A_RANK = 64
G_RANK = 128
N_MOD = 9
RMS_EPS = 1e-6
GN_EPS = 64e-5
VMEM_LIMIT = 56 * 1024 * 1024


def _cparams(*sem):
    return pltpu.CompilerParams(dimension_semantics=sem, vmem_limit_bytes=VMEM_LIMIT)


def _dot(a, b):
    return jnp.dot(a.astype(BF16), b.astype(BF16), preferred_element_type=F32)


def _dot_nt(a, b):
    return lax.dot_general(a.astype(BF16), b.astype(BF16), (((1,), (1,)), ((), ())),
                           preferred_element_type=F32)


def _dot_tn(a, b):
    return lax.dot_general(a.astype(BF16), b.astype(BF16), (((0,), (0,)), ((), ())),
                           preferred_element_type=F32)


def _split2(x):
    hi = x.astype(BF16)
    return hi, (x - hi.astype(F32)).astype(BF16)


def _dot01(m01, x):
    hi, lo = _split2(x)
    return (jnp.dot(m01, hi, preferred_element_type=F32)
            + jnp.dot(m01, lo, preferred_element_type=F32))


def _dot_x01(x, m01):
    hi, lo = _split2(x)
    return (jnp.dot(hi, m01, preferred_element_type=F32)
            + jnp.dot(lo, m01, preferred_element_type=F32))


def _sigmoid(x):
    return 0.5 * jnp.tanh(0.5 * x) + 0.5


def _silu(x):
    return x * _sigmoid(x)


def _rms_mod(x, gain, shift, scale):
    y = x * lax.rsqrt(jnp.mean(x * x, axis=-1, keepdims=True) + RMS_EPS)
    return (y * gain) * (1.0 + scale) + shift


def _mod_kernel(c_ref, w_ref, b_ref, o_ref):
    s = _silu(c_ref[...])
    o_ref[0] = jnp.dot(s, w_ref[0], preferred_element_type=F32,
                       precision=lax.Precision.HIGHEST) + b_ref[0]


def _mod_call(cvec, w_mod, b_mod):
    depth, d, n = w_mod.shape
    tn = 1152 if n % 1152 == 0 else n
    return pl.pallas_call(
        _mod_kernel,
        grid=(depth, n // tn),
        in_specs=[pl.BlockSpec((8, d), lambda l, j: (0, 0)),
                  pl.BlockSpec((1, d, tn), lambda l, j: (l, 0, j)),
                  pl.BlockSpec((1, 1, tn), lambda l, j: (l, 0, j))],
        out_specs=pl.BlockSpec((1, 8, tn), lambda l, j: (l, 0, j)),
        out_shape=jax.ShapeDtypeStruct((depth, 8, n), F32),
        compiler_params=_cparams("arbitrary", "arbitrary"),
        name="mod_vectors",
    )(cvec, w_mod, b_mod.reshape(depth, 1, n))


def _row_tile(n_ctx_rows, seq, cap):
    tm = cap
    while n_ctx_rows % tm or seq % tm:
        tm //= 2
    return tm


def _mod_index(i, ctx_tiles, tiles_per_batch, n_batch):
    return jnp.where(i < ctx_tiles, n_batch, (i - ctx_tiles) // tiles_per_batch)


def _resident(shape):
    return pl.BlockSpec(shape, lambda *_: (0,) * len(shape), pipeline_mode=pl.Buffered(1))


def _ffn_kernel(x_ref, m_ref, g_ref, wg_ref, wu_ref, wd_ref, *rest, mod_base, ff_chunk, final):
    if final:
        fg_ref, o_ref = rest
    else:
        (o_ref,) = rest
    x = x_ref[...]
    m = m_ref[0]
    h = _rms_mod(x, g_ref[...], m[mod_base:mod_base + 1], m[mod_base + 1:mod_base + 2]).astype(BF16)
    d_ff = wd_ref.shape[0]
    acc = jnp.zeros(x.shape, F32)
    for j in range(d_ff // ff_chunk):
        sl = slice(j * ff_chunk, (j + 1) * ff_chunk)
        gate = jnp.dot(h, wg_ref[:, sl], preferred_element_type=F32)
        up = jnp.dot(h, wu_ref[:, sl], preferred_element_type=F32)
        act = (_silu(gate) * up).astype(BF16)
        acc = acc + jnp.dot(act, wd_ref[sl, :], preferred_element_type=F32)
    y = x + (0.5 * m[mod_base + 2:mod_base + 3]) * acc
    if final:
        y = (y * lax.rsqrt(jnp.mean(y * y, axis=-1, keepdims=True) + RMS_EPS)) * fg_ref[...]
    o_ref[...] = y


def _ffn_call(xa, mods, gain, wg, wu, wd, *, mod_base, tm, ctx_tiles, tiles_per_batch, n_batch,
              skip_ctx=False, final_gain=None):
    rows, d = xa.shape
    d_ff = wd.shape[0]
    n_tiles = rows // tm
    t0 = ctx_tiles if skip_ctx else 0
    final = final_gain is not None
    in_specs = [pl.BlockSpec((tm, d), lambda i: (i + t0, 0)),
                pl.BlockSpec((1, N_MOD, d),
                             lambda i: (_mod_index(i + t0, ctx_tiles, tiles_per_batch, n_batch), 0, 0)),
                _resident((1, d)), _resident((d, d_ff)), _resident((d, d_ff)), _resident((d_ff, d))]
    args = [xa, mods, gain.reshape(1, d), wg, wu, wd]
    if final:
        in_specs.append(_resident((1, d)))
        args.append(final_gain.reshape(1, d))
    out_rows = rows - t0 * tm
    return pl.pallas_call(
        functools.partial(_ffn_kernel, mod_base=mod_base, ff_chunk=256, final=final),
        grid=(n_tiles - t0,),
        in_specs=in_specs,
        out_specs=pl.BlockSpec((tm, d), lambda i: (i, 0)),
        out_shape=jax.ShapeDtypeStruct((out_rows, d), F32),
        compiler_params=_cparams("arbitrary"),
        name="ffn_half_step",
    )(*args)


def _inproj_kernel(x_ref, m_ref, g_ref, whg_ref, wrw_ref, ohg_ref, orw_ref):
    m = m_ref[0]
    h = _rms_mod(x_ref[...], g_ref[...], m[3:4], m[4:5]).astype(BF16)
    ohg_ref[...] = jnp.dot(h, whg_ref[...], preferred_element_type=F32)
    orw_ref[...] = jnp.dot(h, wrw_ref[...], preferred_element_type=F32)


def _inproj_call(xa, mods, gain, w_hg, w_rw, *, tm, ctx_tiles, tiles_per_batch, n_batch):
    rows, d = xa.shape
    n_hg, n_rw = w_hg.shape[1], w_rw.shape[1]
    return pl.pallas_call(
        _inproj_kernel,
        grid=(rows // tm,),
        in_specs=[pl.BlockSpec((tm, d), lambda i: (i, 0)),
                  pl.BlockSpec((1, N_MOD, d),
                               lambda i: (_mod_index(i, ctx_tiles, tiles_per_batch, n_batch), 0, 0)),
                  _resident((1, d)), _resident((d, n_hg)), _resident((d, n_rw))],
        out_specs=[pl.BlockSpec((tm, n_hg), lambda i: (i, 0)),
                   pl.BlockSpec((tm, n_rw), lambda i: (i, 0))],
        out_shape=[jax.ShapeDtypeStruct((rows, n_hg), F32), jax.ShapeDtypeStruct((rows, n_rw), F32)],
        compiler_params=_cparams("arbitrary"),
        name="in_projection",
    )(xa, mods, gain.reshape(1, d), w_hg, w_rw)


def _chunk_maps(n_batch, nc, nl):
    nt = nc + nl

    def rb(b, c):
        return jnp.where(c < nc, b * nc + c, n_batch * nc + b * nl + (c - nc))

    def c_fwd(i):
        return i

    def c_bwd(i):
        return jnp.where(i < nc, nc - 1 - i, nt + nc - 1 - i)

    def c_prev(c):
        return jnp.where(c < nc, jnp.maximum(c - 1, 0), jnp.maximum(c - 1, nc))

    def c_next(c):
        return jnp.where(c < nc, jnp.minimum(c + 1, nc - 1), jnp.minimum(c + 1, nt - 1))

    return rb, c_fwd, c_bwd, c_prev, c_next


def _interleave(*stages):
    values = [None] * len(stages)
    live = list(range(len(stages)))
    while live:
        for g in tuple(live):
            try:
                next(stages[g])
            except StopIteration as stop:
                values[g] = stop.value
                live.remove(g)
    return values


def _tri(rev, strict):
    r = lax.broadcasted_iota(jnp.int32, (CHUNK, CHUNK), 0)
    c = lax.broadcasted_iota(jnp.int32, (CHUNK, CHUNK), 1)
    if rev:
        return (r < c) if strict else (r <= c)
    return (r > c) if strict else (r >= c)


def _gla_tables():
    n = CHUNK
    idx = np.arange(n)
    incl = (idx[:, None] >= idx[None, :])
    rows = [incl, (idx[None, :] > idx[:, None])]
    masks = [np.eye(n, dtype=bool)]
    h = n // 2
    while h >= 1:
        blk = idx // (2 * h)
        mid = blk * 2 * h + h
        rowpart = (idx % (2 * h)) >= h
        as_row = (idx[None, :] > mid[:, None]) & (idx[None, :] <= idx[:, None]) & rowpart[:, None]
        as_col = (idx[None, :] > idx[:, None]) & (idx[None, :] <= mid[:, None]) & (~rowpart)[:, None]
        rows.append(as_row | as_col)
        masks.append((blk[:, None] == blk[None, :]) & rowpart[:, None] & (~rowpart)[None, :])
        h //= 2
    fwd = np.concatenate(rows, axis=0).astype(np.float32)
    mask_f = np.stack(masks).astype(np.float32)
    n_blocks = fwd.shape[0] // n
    rev = fwd.reshape(n_blocks, n, n)[:, ::-1, ::-1].reshape(fwd.shape)
    mask_r = mask_f[:, ::-1, ::-1]
    return np.stack([fwd, rev]), np.stack([mask_f, mask_r])


def _gla_gates(inputs, lbp_ref, tab_ref, g_ref, e_ref):
    dk = g_ref.shape[3] // HGRN_HEADS
    for d, (q_ref, z_ref, v_ref) in enumerate(inputs):
        for h in range(HGRN_HEADS):
            sl = slice(h * dk, (h + 1) * dk)
            lbp = lbp_ref[d, :, sl]
            log_lb, log_1mlb, one_mlb = lbp[0:1], lbp[1:2], lbp[2:3]
            z = z_ref[:, sl]
            log_sig = jnp.minimum(z, 0.0) - jnp.log(1.0 + jnp.exp(-jnp.abs(z)))
            bv = log_1mlb + log_sig
            lf = jnp.maximum(log_lb, bv) + jnp.log(1.0 + jnp.exp(-jnp.abs(log_lb - bv)))
            g_ref[d, 0, :, sl] = _silu(q_ref[:, sl])
            g_ref[d, 1, :, sl] = one_mlb * _sigmoid(-z)
            g_ref[d, 2, :, sl] = v_ref[:, sl]
            e_ref[d, :, sl] = jnp.exp(_dot01(tab_ref[d], lf))
            yield


def _gla_scan(g_ref, e_ref, mask_ref, s_ref):
    c = CHUNK
    n_levels = mask_ref.shape[1] - 1
    bf = lambda t: t.astype(BF16)
    mm = lambda a, b: jnp.dot(a, b, preferred_element_type=F32)
    mm_nt = lambda a, b: lax.dot_general(a, b, (((1,), (1,)), ((), ())), preferred_element_type=F32)
    mm_tn = lambda a, b: lax.dot_general(a, b, (((0,), (0,)), ((), ())), preferred_element_type=F32)
    dk = g_ref.shape[3] // HGRN_HEADS
    chains = []
    for d in range(2):
        masks = mask_ref[d]
        for h in range(HGRN_HEADS):
            sl = slice(h * dk, (h + 1) * dk)
            qh, kh, ex = g_ref[d, 0, :, sl], g_ref[d, 1, :, sl], e_ref[d, :, sl]
            ch = dict(d=d, h=h, v=bf(g_ref[d, 2, :, sl]),
                      q_in=bf(qh * ex[0:c]), k_out=bf(kh * ex[c:2 * c]),
                      w_tot=ex[0:1] if d == 1 else ex[c - 1:c])
            att = masks[0] * mm_nt(bf(qh), bf(kh))
            for l in range(n_levels):
                lv = ex[(2 + l) * c:(3 + l) * c]
                att = att + masks[l + 1] * mm_nt(bf(qh * lv), bf(kh * lv))
            ch["att"] = bf(att)
            chains.append(ch)
            yield
    outs = [[None] * HGRN_HEADS for _ in range(2)]
    for ch in chains:
        st = s_ref[ch["d"], ch["h"]]
        outs[ch["d"]][ch["h"]] = mm(ch["att"], ch["v"]) + mm_nt(ch["q_in"], bf(st))
        s_ref[ch["d"], ch["h"]] = st * ch["w_tot"] + mm_tn(ch["v"], ch["k_out"])
    return [jnp.concatenate(o, axis=1) for o in outs]


def _shift_conv(prev_ref, cur_ref, nxt_ref, conv_ref, out_ref, is_lat, has_prev, has_next):
    n, width = cur_ref.shape
    row = lax.broadcasted_iota(jnp.int32, (n, LANES), 0)
    lat = jnp.where(is_lat, 1.0, 0.0)
    hp = jnp.where(has_prev, 1.0, 0.0)
    hn = jnp.where(has_next, 1.0, 0.0)
    edge_l = (1.0 - lat) * hp
    edge_r = (1.0 - lat) * hn
    for s in range(width // LANES):
        sl = slice(s * LANES, (s + 1) * LANES)
        kern = conv_ref[:, sl]
        prev, cur, nxt = prev_ref[:, sl], cur_ref[:, sl], nxt_ref[:, sl]
        k_up = kern[0:3] * (lat * hp)
        k_dn = kern[6:9] * (lat * hn)

        def column(j):
            return prev * k_up[j:j + 1] + cur * kern[3 + j:4 + j] + nxt * k_dn[j:j + 1]

        fill_l = prev[n - 1:n] * (edge_l * kern[3:4])
        fill_r = nxt[0:1] * (edge_r * kern[5:6])
        out_ref[:, sl] = (column(1)
                          + jnp.where(row == 0, fill_l, pltpu.roll(column(0), 1, 0))
                          + jnp.where(row == n - 1, fill_r, pltpu.roll(column(2), n - 1, 0)))
        if s % 5 == 4:
            yield


def _pair_diag(x):
    lo = lax.broadcasted_iota(jnp.int32, x.shape, 1) < RWKV_DH
    z = jnp.zeros_like(x)
    return jnp.concatenate([jnp.where(lo, x, z), jnp.where(lo, z, x)], axis=0)


def _rwkv_scan_chunks(q_ref, s_ref):
    c = CHUNK
    pw = 2 * RWKV_DH
    n_pairs = q_ref.shape[3] // pw
    row = lax.broadcasted_iota(jnp.int32, (c, pw), 0)
    lane = lax.broadcasted_iota(jnp.int32, (c, pw), 1)
    col = lane & (RWKV_DH - 1)
    lo = lane < RWKV_DH
    eye = row == col
    bf = lambda t: t.astype(BF16)
    mm = lambda a, b: jnp.dot(a, b, preferred_element_type=F32)
    mm_nt = lambda a, b: lax.dot_general(a, b, (((1,), (1,)), ((), ())), preferred_element_type=F32)
    mm_tn = lambda a, b: lax.dot_general(a, b, (((0,), (0,)), ((), ())), preferred_element_type=F32)

    chains = []
    for d in range(2):
        rev = d == 1
        tri = (row <= col) if rev else (row >= col)
        strict = (row < col) if rev else (row > col)
        tri01 = jnp.where(_tri(rev, False), 1.0, 0.0).astype(BF16)
        for j in range(n_pairs):
            sl = slice(j * pw, (j + 1) * pw)
            r, lw, k, v, kk, a = (q_ref[d, t, :, sl] for t in range(6))
            b_incl = _dot01(tri01, lw)
            b_tot = b_incl[0:1] if rev else b_incl[c - 1:c]
            e_in = jnp.exp(-b_incl)
            e_out = jnp.exp(b_tot - b_incl)
            beta = kk * a
            chains.append(dict(
                d=d, j=j, tri=tri, strict=strict, w_tot=jnp.exp(b_tot), v=v,
                ar=bf(jnp.concatenate([-kk * jnp.exp(b_incl - lw), r * jnp.exp(b_incl)], axis=0)),
                bt=_pair_diag(bf(beta * e_in)), kt=_pair_diag(bf(k * e_in)),
                bk=bf(jnp.concatenate([beta * e_out, k * e_out], axis=0))))
    yield
    for ch in chains:
        g_b = mm_nt(ch["ar"], ch["bt"])
        g_k = mm_nt(ch["ar"], ch["kt"])
        a_ab = jnp.where(ch["strict"], g_b[0:c], 0.0)
        ch["a_rb"] = bf(jnp.where(ch["tri"], g_b[c:], 0.0))
        ch["a_k"] = bf(jnp.concatenate([jnp.where(ch["strict"], g_k[0:c], 0.0),
                                        jnp.where(ch["tri"], g_k[c:], 0.0)], axis=0))
        ch["p"] = jnp.where(eye, 1.0, a_ab)
        ch["ak"] = bf(a_ab)
    yield
    for ch in chains:
        ch["ak2"] = mm(ch["ak"], _pair_diag(ch["ak"]))
    yield
    n_lv = int(math.log2(c))
    for lv in range(2, n_lv + 1):
        for ch in chains:
            ak = bf(ch["ak2"])
            p_bd = _pair_diag(bf(ch["p"]))
            if lv < n_lv:
                both = mm(ak, jnp.concatenate([_pair_diag(ak), p_bd], axis=1))
                ch["ak2"] = both[:, :pw]
                ch["p"] = ch["p"] + both[:, pw:]
            else:
                ch["p"] = ch["p"] + mm(ak, p_bd)
        yield
    for ch in chains:
        s0 = s_ref[ch["d"], ch["j"]]
        ch["s0"] = s0
        ch["xs"] = mm_nt(ch["ar"], _pair_diag(bf(s0)))
        ch["av"] = mm(ch["a_k"], _pair_diag(bf(ch["v"])))
    yield
    for ch in chains:
        x = ch["xs"][0:c] + ch["av"][0:c]
        ch["u"] = mm(bf(ch["p"]), _pair_diag(bf(x)))
    yield
    outs = [[None] * n_pairs for _ in range(2)]
    for ch in chains:
        u = ch["u"]
        outs[ch["d"]][ch["j"]] = ch["xs"][c:] + ch["av"][c:] + mm(ch["a_rb"], _pair_diag(bf(u)))
        z = mm_tn(bf(jnp.concatenate([u, ch["v"]], axis=0)), ch["bk"])
        s_ref[ch["d"], ch["j"]] = ch["s0"] * ch["w_tot"] + jnp.where(lo, z[0:c], z[c:])
    return [jnp.concatenate(o, axis=1) for o in outs]


def _rwkv_prep(rw_ref, vec_ref, wup_ref, aup_ref, gup_ref, bd_ref, q_ref, bonus_ref, gr_ref, d):
    dr = q_ref.shape[3]
    o_w = 3 * dr
    o_a = o_w + 2 * W_RANK
    o_g = o_a + 2 * A_RANK
    bd = bd_ref[0:LANES, 0:LANES]
    tanh_wd = jnp.tanh(rw_ref[:, o_w + d * W_RANK:o_w + (d + 1) * W_RANK]).astype(BF16)
    ad = [rw_ref[:, o_a + e * A_RANK:o_a + (e + 1) * A_RANK].astype(BF16) for e in range(2)]
    if d == 0:
        sig_gd = _sigmoid(rw_ref[:, o_g:o_g + G_RANK]).astype(BF16)
    mm = lambda a, b: jnp.dot(a, b, preferred_element_type=F32)
    for j in range(dr // LANES):
        sl = slice(j * LANES, (j + 1) * LANES)
        vec = vec_ref[:, sl]
        w0, a0 = vec[0:2], vec[2:4]
        k_k, k_a, r_k = vec[4:5], vec[5:6], vec[6:7]
        r = rw_ref[:, j * LANES:(j + 1) * LANES]
        k = rw_ref[:, dr + j * LANES:dr + (j + 1) * LANES]
        v = rw_ref[:, 2 * dr + j * LANES:2 * dr + (j + 1) * LANES]
        lw = -math.exp(-0.5) * _sigmoid(w0[d:d + 1] + mm(tanh_wd, wup_ref[d, :, sl]))
        a = _sigmoid(a0[d:d + 1] + mm(ad[d], aup_ref[d, :, sl]))
        kk = k * k_k
        kk = kk / jnp.maximum(jnp.sqrt(_dot_x01(kk * kk, bd)), 1e-12)
        for t, val in enumerate((r, lw, k * (1.0 + (a - 1.0) * k_a), v, kk, a)):
            q_ref[d, t, :, sl] = val
        if d == 0:
            a_b = _sigmoid(a0[1:2] + mm(ad[1], aup_ref[1, :, sl]))
            k_sum = k * (2.0 + (a + a_b - 2.0) * k_a)
            bonus_ref[:, sl] = _dot_x01(r * k_sum * r_k, bd) * v
            gr_ref[:, sl] = mm(sig_gd, gup_ref[:, sl])
        if j % 2 == 1:
            yield


def _mix_kernel(qf_ref, zf_ref, vf_ref, qb_ref, zb_ref, vb_ref, lbp_ref, tab_ref, mask_ref,
                pf_ref, cf_ref, nf_ref, pb_ref, cb_ref, nb_ref, conv_ref, vec_ref, wup_ref, aup_ref, gup_ref,
                bd_ref, of_ref, ob_ref, yf_ref, yb_ref, bonus_ref, gr_ref,
                hs_ref, g_ref, e_ref, s_ref, q_ref, rw_ref, *, nc, nt):
    step = pl.program_id(1)

    @pl.when(step == 0)
    def _():
        g_ref[...] = jnp.zeros(g_ref.shape, F32)
        e_ref[...] = jnp.zeros(e_ref.shape, F32)
        q_ref[...] = jnp.zeros(q_ref.shape, F32)

    @pl.when(step <= 1)
    def _():
        hs_ref[...] = jnp.zeros(hs_ref.shape, F32)
        s_ref[...] = jnp.zeros(s_ref.shape, F32)

    scan = _rwkv_scan_chunks(q_ref, s_ref)
    next(scan)

    i = jnp.minimum(step, nt - 1)

    def flags(c):
        is_lat = c >= nc
        has_prev = jnp.logical_and(c != 0, c != nc)
        has_next = jnp.logical_and(c != nc - 1, c != nt - 1)
        return is_lat, has_prev, has_next

    def prepare():
        chunk = (i, jnp.where(i < nc, nc - 1 - i, nt + nc - 1 - i))
        blocks = ((pf_ref, cf_ref, nf_ref), (pb_ref, cb_ref, nb_ref))
        for d in range(2):
            yield from _shift_conv(*blocks[d], conv_ref, rw_ref, *flags(chunk[d]))
            yield from _rwkv_prep(rw_ref, vec_ref, wup_ref, aup_ref, gup_ref, bd_ref, q_ref, bonus_ref,
                                  gr_ref, d)

    (y_f, y_b), _, (o_f, o_b), _ = _interleave(
        scan, prepare(), _gla_scan(g_ref, e_ref, mask_ref, hs_ref),
        _gla_gates(((qf_ref, zf_ref, vf_ref), (qb_ref, zb_ref, vb_ref)), lbp_ref, tab_ref, g_ref, e_ref))
    yf_ref[...] = y_f
    yb_ref[...] = y_b
    of_ref[...] = o_f
    ob_ref[...] = o_b


def _mix_call(p_hg, p_rw, lbp, conv, vec, w_up, a_up, g_up, bd, *, n_batch, nc, nl):
    rows, n_rw = p_rw.shape
    dh = p_hg.shape[1] // 5
    dr = bd.shape[0]
    nt = nc + nl
    rb, c_fwd, c_bwd, c_prev, c_next = _chunk_maps(n_batch, nc, nl)
    tab, masks = _gla_tables()
    tab = jnp.asarray(tab, BF16)
    masks = jnp.asarray(masks, F32)
    nxt = lambda i: jnp.minimum(i, nt - 1)
    cur = lambda i: jnp.maximum(i - 1, 0)
    ident = lambda c: c

    def hg(col, cmap):
        return pl.BlockSpec((CHUNK, dh), lambda b, i: (rb(b, cmap(nxt(i))), col))

    def rw(cmap, nb):
        return pl.BlockSpec((CHUNK, n_rw), lambda b, i: (rb(b, nb(cmap(nxt(i)))), 0))

    def out(width, cmap, when):
        return pl.BlockSpec((CHUNK, width), lambda b, i: (rb(b, cmap(when(i))), 0))

    dk = dh // HGRN_HEADS
    n_heads = dr // RWKV_DH
    consts = (lbp, tab, masks)
    consts2 = (conv, vec, w_up, a_up, g_up, bd)
    return pl.pallas_call(
        functools.partial(_mix_kernel, nc=nc, nt=nt),
        grid=(n_batch, nt + 1),
        in_specs=[hg(0, c_fwd), hg(1, c_fwd), hg(3, c_fwd), hg(0, c_bwd), hg(2, c_bwd), hg(3, c_bwd)]
        + [_resident(a.shape) for a in consts]
        + [rw(c_fwd, c_prev), rw(c_fwd, ident), rw(c_fwd, c_next),
           rw(c_bwd, c_prev), rw(c_bwd, ident), rw(c_bwd, c_next)]
        + [_resident(a.shape) for a in consts2],
        out_specs=[out(dh, c_fwd, cur), out(dh, c_bwd, cur), out(dr, c_fwd, cur), out(dr, c_bwd, cur),
                   out(dr, c_fwd, nxt), out(dr, c_fwd, nxt)],
        out_shape=[jax.ShapeDtypeStruct((rows, dh), F32)] * 2 + [jax.ShapeDtypeStruct((rows, dr), F32)] * 4,
        scratch_shapes=[pltpu.VMEM((2, HGRN_HEADS, dk, dk), F32),
                        pltpu.VMEM((2, 3, CHUNK, dh), F32),
                        pltpu.VMEM((2, tab.shape[1], dh), F32),
                        pltpu.VMEM((2, n_heads // 2, RWKV_DH, 2 * RWKV_DH), F32),
                        pltpu.VMEM((2, 6, CHUNK, dr), F32),
                        pltpu.VMEM((CHUNK, n_rw), F32)],
        compiler_params=_cparams("arbitrary", "arbitrary"),
        name="token_mix_scans",
    )(p_hg, p_hg, p_hg, p_hg, p_hg, p_hg, *consts, p_rw, p_rw, p_rw, p_rw, p_rw, p_rw, *consts2)


def _post_kernel(x_ref, m_ref, of_ref, ob_ref, go_ref, yf_ref, yb_ref, bonus_ref, gr_ref, vec_ref,
                 bd_ref, wo_ref, o_ref):
    vec = vec_ref[...]
    o_gain, ln_w, ln_b = vec[0:1], vec[1:2], vec[2:3]
    o = of_ref[...] + ob_ref[...]
    dh = o.shape[1] // HGRN_HEADS
    parts = []
    for h in range(HGRN_HEADS):
        oh = o[:, h * dh:(h + 1) * dh]
        parts.append(oh * lax.rsqrt(jnp.mean(oh * oh, axis=-1, keepdims=True) + RMS_EPS))
    o = jnp.concatenate(parts, axis=1) * o_gain * _silu(go_ref[...])
    y = yf_ref[...] + yb_ref[...]
    bd = bd_ref[...]
    inv = 1.0 / RWKV_DH
    mu = _dot_x01(y, bd) * inv
    yc = y - mu
    var = _dot_x01(yc * yc, bd) * inv
    y = yc * lax.rsqrt(var + GN_EPS) * ln_w + ln_b
    y = (y + bonus_ref[...]) * gr_ref[...]
    mixed = jnp.concatenate([o, y], axis=1).astype(BF16)
    out = jnp.dot(mixed, wo_ref[...], preferred_element_type=F32)
    o_ref[...] = x_ref[...] + m_ref[0][5:6] * out


def _post_call(xa, mods, o_f, o_b, p_hg, y_f, y_b, bonus, g_r, vec, bd, w_out, *, tm, ctx_tiles,
               tiles_per_batch, n_batch):
    rows, d = xa.shape
    dh = o_f.shape[1]
    dr = y_f.shape[1]
    row = lambda w: pl.BlockSpec((tm, w), lambda i: (i, 0))
    return pl.pallas_call(
        _post_kernel,
        grid=(rows // tm,),
        in_specs=[row(d),
                  pl.BlockSpec((1, N_MOD, d),
                               lambda i: (_mod_index(i, ctx_tiles, tiles_per_batch, n_batch), 0, 0)),
                  row(dh), row(dh), pl.BlockSpec((tm, dh), lambda i: (i, 4)),
                  row(dr), row(dr), row(dr), row(dr),
                  _resident(vec.shape), _resident(bd.shape), _resident(w_out.shape)],
        out_specs=row(d),
        out_shape=jax.ShapeDtypeStruct((rows, d), F32),
        compiler_params=_cparams("arbitrary"),
        name="mix_out_projection",
    )(xa, mods, o_f, o_b, p_hg, y_f, y_b, bonus, g_r, vec, bd, w_out)


def kernel(x, c, ctx, c_ctx, w_mod, b_mod, norm_gains, final_gain, ffn1_up, ffn1_down, ffn2_up, ffn2_down, w_in, w_out, hgrn_lb_logits, hgrn_o_gain, rwkv_conv, rwkv_w0, rwkv_w_up, rwkv_a0, rwkv_a_up, rwkv_g_up, rwkv_k_k, rwkv_k_a, rwkv_r_k, rwkv_ln_w, rwkv_ln_b):
    n_batch, seq, d = x.shape
    n_ctx = ctx.shape[1]
    depth = w_mod.shape[0]
    d_ff = ffn1_down.shape[1]
    d_hgrn = hgrn_o_gain.shape[1]
    d_rwkv = rwkv_k_k.shape[1]
    hg_cols = 5 * d_hgrn
    assert seq % CHUNK == 0 and n_ctx % CHUNK == 0 and GRID_W == CHUNK and n_batch < 8
    nc, nl = n_ctx // CHUNK, seq // CHUNK

    p = jax.nn.softmax(hgrn_lb_logits.astype(F32), axis=0)
    cum = jnp.cumsum(p, axis=0)
    lb = cum - cum[0]
    lbp = jnp.stack([jnp.log(lb), jnp.log1p(-lb), 1.0 - lb], axis=2)

    cvec = jnp.concatenate([c, c_ctx[None], jnp.zeros((7 - n_batch, d), F32)], axis=0)
    mods = _mod_call(cvec, w_mod, b_mod).reshape(depth, 8, N_MOD, d)

    bd = jnp.asarray(np.kron(np.eye(d_rwkv // RWKV_DH), np.ones((RWKV_DH, RWKV_DH))), BF16)

    xa = jnp.concatenate([ctx.reshape(n_batch * n_ctx, d), x.reshape(n_batch * seq, d)], axis=0)
    tm = _row_tile(n_batch * n_ctx, seq, 1024)
    tm_s = _row_tile(n_batch * n_ctx, seq, 512)
    tiles = dict(tm=tm, ctx_tiles=n_batch * n_ctx // tm, tiles_per_batch=seq // tm, n_batch=n_batch)
    tiles_s = dict(tm=tm_s, ctx_tiles=n_batch * n_ctx // tm_s, tiles_per_batch=seq // tm_s, n_batch=n_batch)

    for l in range(depth):
        last = l == depth - 1
        bf = lambda w: w.astype(BF16)
        xa = _ffn_call(xa, mods[l], norm_gains[l, 0], bf(ffn1_up[l, :, :d_ff]), bf(ffn1_up[l, :, d_ff:]),
                       bf(ffn1_down[l]), mod_base=0, **tiles)
        p_hg, p_rw = _inproj_call(xa, mods[l], norm_gains[l, 1], bf(w_in[l, :, :hg_cols]),
                                  bf(w_in[l, :, hg_cols:]), **tiles_s)
        vec = jnp.concatenate([rwkv_w0[l], rwkv_a0[l], rwkv_k_k[l][None], rwkv_k_a[l][None],
                               rwkv_r_k[l].reshape(1, d_rwkv), jnp.zeros((1, d_rwkv), F32)], axis=0)
        o_f, o_b, y_f, y_b, bonus, g_r = _mix_call(
            p_hg, p_rw, lbp[l], rwkv_conv[l].reshape(9, -1), vec, bf(rwkv_w_up[l]), bf(rwkv_a_up[l]),
            bf(rwkv_g_up[l]), bd, n_batch=n_batch, nc=nc, nl=nl)
        vec2 = jnp.concatenate([hgrn_o_gain[l][None], rwkv_ln_w[l][None], rwkv_ln_b[l][None],
                                jnp.zeros((5, d_rwkv), F32)], axis=0)
        xa = _post_call(xa, mods[l], o_f, o_b, p_hg, y_f, y_b, bonus, g_r, vec2, bd, bf(w_out[l]),
                        **tiles_s)
        up, down = bf(ffn2_up[l]), bf(ffn2_down[l])
        xa = _ffn_call(xa, mods[l], norm_gains[l, 2], up[:, :d_ff], up[:, d_ff:], down, mod_base=6,
                       skip_ctx=last, final_gain=final_gain if last else None, **tiles)
    return xa.reshape(n_batch, seq, d)
```

```python
import functools
import math

import numpy as np
import jax
import jax.numpy as jnp
from jax import lax
from jax.experimental import pallas as pl
from jax.experimental.pallas import tpu as pltpu

F32 = jnp.float32
BF16 = jnp.bfloat16

CHUNK = 64
GRID_W = 64
HGRN_HEADS = 4
RWKV_DH = 64
LANES = 128
W_RANK = 64
A_RANK = 64
G_RANK = 128
N_MOD = 9
RMS_EPS = 1e-6
GN_EPS = 64e-5
VMEM_LIMIT = 56 * 1024 * 1024


def _cparams(*sem):
    return pltpu.CompilerParams(dimension_semantics=sem, vmem_limit_bytes=VMEM_LIMIT)


def _dot(a, b):
    return jnp.dot(a.astype(BF16), b.astype(BF16), preferred_element_type=F32)


def _dot_nt(a, b):
    return lax.dot_general(a.astype(BF16), b.astype(BF16), (((1,), (1,)), ((), ())),
                           preferred_element_type=F32)


def _dot_tn(a, b):
    return lax.dot_general(a.astype(BF16), b.astype(BF16), (((0,), (0,)), ((), ())),
                           preferred_element_type=F32)


def _split2(x):
    hi = x.astype(BF16)
    return hi, (x - hi.astype(F32)).astype(BF16)


def _dot01(m01, x):
    hi, lo = _split2(x)
    return (jnp.dot(m01, hi, preferred_element_type=F32)
            + jnp.dot(m01, lo, preferred_element_type=F32))


def _dot_x01(x, m01):
    hi, lo = _split2(x)
    return (jnp.dot(hi, m01, preferred_element_type=F32)
            + jnp.dot(lo, m01, preferred_element_type=F32))


def _sigmoid(x):
    return 0.5 * jnp.tanh(0.5 * x) + 0.5


def _silu(x):
    return x * _sigmoid(x)


def _rms_mod(x, gain, shift, scale):
    y = x * lax.rsqrt(jnp.mean(x * x, axis=-1, keepdims=True) + RMS_EPS)
    return (y * gain) * (1.0 + scale) + shift


def _mod_kernel(c_ref, w_ref, b_ref, o_ref):
    s = _silu(c_ref[...])
    o_ref[0] = jnp.dot(s, w_ref[0], preferred_element_type=F32,
                       precision=lax.Precision.HIGHEST) + b_ref[0]


def _mod_call(cvec, w_mod, b_mod):
    depth, d, n = w_mod.shape
    tn = 1152 if n % 1152 == 0 else n
    return pl.pallas_call(
        _mod_kernel,
        grid=(depth, n // tn),
        in_specs=[pl.BlockSpec((8, d), lambda l, j: (0, 0)),
                  pl.BlockSpec((1, d, tn), lambda l, j: (l, 0, j)),
                  pl.BlockSpec((1, 1, tn), lambda l, j: (l, 0, j))],
        out_specs=pl.BlockSpec((1, 8, tn), lambda l, j: (l, 0, j)),
        out_shape=jax.ShapeDtypeStruct((depth, 8, n), F32),
        compiler_params=_cparams("arbitrary", "arbitrary"),
        name="mod_vectors",
    )(cvec, w_mod, b_mod.reshape(depth, 1, n))


def _row_tile(n_ctx_rows, seq, cap):
    tm = cap
    while n_ctx_rows % tm or seq % tm:
        tm //= 2
    return tm


def _mod_index(i, ctx_tiles, tiles_per_batch, n_batch):
    return jnp.where(i < ctx_tiles, n_batch, (i - ctx_tiles) // tiles_per_batch)


def _resident(shape):
    return pl.BlockSpec(shape, lambda *_: (0,) * len(shape), pipeline_mode=pl.Buffered(1))


def _ffn_kernel(x_ref, m_ref, g_ref, wg_ref, wu_ref, wd_ref, *rest, mod_base, ff_chunk, final):
    if final:
        fg_ref, o_ref = rest
    else:
        (o_ref,) = rest
    x = x_ref[...]
    m = m_ref[0]
    h = _rms_mod(x, g_ref[...], m[mod_base:mod_base + 1], m[mod_base + 1:mod_base + 2]).astype(BF16)
    d_ff = wd_ref.shape[0]
    acc = jnp.zeros(x.shape, F32)
    for j in range(d_ff // ff_chunk):
        sl = slice(j * ff_chunk, (j + 1) * ff_chunk)
        gate = jnp.dot(h, wg_ref[:, sl], preferred_element_type=F32)
        up = jnp.dot(h, wu_ref[:, sl], preferred_element_type=F32)
        act = (_silu(gate) * up).astype(BF16)
        acc = acc + jnp.dot(act, wd_ref[sl, :], preferred_element_type=F32)
    y = x + (0.5 * m[mod_base + 2:mod_base + 3]) * acc
    if final:
        y = (y * lax.rsqrt(jnp.mean(y * y, axis=-1, keepdims=True) + RMS_EPS)) * fg_ref[...]
    o_ref[...] = y


def _ffn_call(xa, mods, gain, wg, wu, wd, *, mod_base, tm, ctx_tiles, tiles_per_batch, n_batch,
              skip_ctx=False, final_gain=None):
    rows, d = xa.shape
    d_ff = wd.shape[0]
    n_tiles = rows // tm
    t0 = ctx_tiles if skip_ctx else 0
    final = final_gain is not None
    in_specs = [pl.BlockSpec((tm, d), lambda i: (i + t0, 0)),
                pl.BlockSpec((1, N_MOD, d),
                             lambda i: (_mod_index(i + t0, ctx_tiles, tiles_per_batch, n_batch), 0, 0)),
                _resident((1, d)), _resident((d, d_ff)), _resident((d, d_ff)), _resident((d_ff, d))]
    args = [xa, mods, gain.reshape(1, d), wg, wu, wd]
    if final:
        in_specs.append(_resident((1, d)))
        args.append(final_gain.reshape(1, d))
    out_rows = rows - t0 * tm
    return pl.pallas_call(
        functools.partial(_ffn_kernel, mod_base=mod_base, ff_chunk=256, final=final),
        grid=(n_tiles - t0,),
        in_specs=in_specs,
        out_specs=pl.BlockSpec((tm, d), lambda i: (i, 0)),
        out_shape=jax.ShapeDtypeStruct((out_rows, d), F32),
        compiler_params=_cparams("arbitrary"),
        name="ffn_half_step",
    )(*args)


def _inproj_kernel(x_ref, m_ref, g_ref, whg_ref, wrw_ref, conv_ref, ohg_ref, orw_ref, tile_ref, edge_ref,
                   *, ctx_chunks, nc, nl):
    step = pl.program_id(0)
    tm = x_ref.shape[0]
    cpt = tm // CHUNK
    new = tile_ref.at[step % 2]
    old = tile_ref.at[(step + 1) % 2]

    @pl.when(step == 0)
    def _():
        tile_ref[...] = jnp.zeros(tile_ref.shape, F32)
        edge_ref[...] = jnp.zeros(edge_ref.shape, F32)

    m = m_ref[0]
    h = _rms_mod(x_ref[...], g_ref[...], m[3:4], m[4:5]).astype(BF16)

    def project():
        for w_ref, o_ref, n_chunks in ((wrw_ref, new, 5), (whg_ref, ohg_ref, 5)):
            width = w_ref.shape[1] // n_chunks
            for j in range(n_chunks):
                sl = slice(j * width, (j + 1) * width)
                o_ref[:, sl] = jnp.dot(h, w_ref[:, sl], preferred_element_type=F32)
                yield

    def shift():
        first = jnp.maximum(step - 1, 0) * cpt
        yield
        for c in range(cpt):
            g = first + c
            is_lat = g >= ctx_chunks
            pos = jnp.where(is_lat, (g - ctx_chunks) % nl, g % nc)
            has_prev = pos != 0
            has_next = pos != jnp.where(is_lat, nl - 1, nc - 1)
            prev = old.at[(c - 1) * CHUNK:c * CHUNK] if c else edge_ref
            nxt = old.at[(c + 1) * CHUNK:(c + 2) * CHUNK] if c < cpt - 1 else new.at[0:CHUNK]
            yield from _shift_conv(prev, old.at[c * CHUNK:(c + 1) * CHUNK], nxt, conv_ref,
                                   orw_ref.at[c * CHUNK:(c + 1) * CHUNK], is_lat, has_prev, has_next)

    _interleave(project(), shift())
    edge_ref[...] = old[tm - CHUNK:tm]


def _inproj_call(xa, mods, gain, w_hg, w_rw, conv, *, tm, ctx_tiles, tiles_per_batch, n_batch, nc, nl):
    rows, d = xa.shape
    n_hg, n_rw = w_hg.shape[1], w_rw.shape[1]
    last = rows // tm - 1
    cur = lambda i: jnp.minimum(i, last)
    return pl.pallas_call(
        functools.partial(_inproj_kernel, ctx_chunks=n_batch * nc, nc=nc, nl=nl),
        grid=(rows // tm + 1,),
        in_specs=[pl.BlockSpec((tm, d), lambda i: (cur(i), 0)),
                  pl.BlockSpec((1, N_MOD, d),
                               lambda i: (_mod_index(cur(i), ctx_tiles, tiles_per_batch, n_batch), 0, 0)),
                  _resident((1, d)), _resident((d, n_hg)), _resident((d, n_rw)), _resident(conv.shape)],
        out_specs=[pl.BlockSpec((tm, n_hg), lambda i: (cur(i), 0)),
                   pl.BlockSpec((tm, n_rw), lambda i: (jnp.maximum(i - 1, 0), 0))],
        out_shape=[jax.ShapeDtypeStruct((rows, n_hg), F32), jax.ShapeDtypeStruct((rows, n_rw), F32)],
        scratch_shapes=[pltpu.VMEM((2, tm, n_rw), F32), pltpu.VMEM((CHUNK, n_rw), F32)],
        compiler_params=_cparams("arbitrary"),
        name="in_projection",
    )(xa, mods, gain.reshape(1, d), w_hg, w_rw, conv)


def _chunk_maps(n_batch, nc, nl):
    nt = nc + nl

    def rb(b, c):
        return jnp.where(c < nc, b * nc + c, n_batch * nc + b * nl + (c - nc))

    def c_fwd(i):
        return i

    def c_bwd(i):
        return jnp.where(i < nc, nc - 1 - i, nt + nc - 1 - i)

    return rb, c_fwd, c_bwd


def _interleave(*stages):
    values = [None] * len(stages)
    live = list(range(len(stages)))
    while live:
        for g in tuple(live):
            try:
                next(stages[g])
            except StopIteration as stop:
                values[g] = stop.value
                live.remove(g)
    return values


def _tri(rev, strict):
    r = lax.broadcasted_iota(jnp.int32, (CHUNK, CHUNK), 0)
    c = lax.broadcasted_iota(jnp.int32, (CHUNK, CHUNK), 1)
    if rev:
        return (r < c) if strict else (r <= c)
    return (r > c) if strict else (r >= c)


def _gla_tables():
    n = CHUNK
    idx = np.arange(n)
    incl = (idx[:, None] >= idx[None, :])
    rows = [incl, (idx[None, :] > idx[:, None])]
    masks = [np.eye(n, dtype=bool)]
    h = n // 2
    while h >= 1:
        blk = idx // (2 * h)
        mid = blk * 2 * h + h
        rowpart = (idx % (2 * h)) >= h
        as_row = (idx[None, :] > mid[:, None]) & (idx[None, :] <= idx[:, None]) & rowpart[:, None]
        as_col = (idx[None, :] > idx[:, None]) & (idx[None, :] <= mid[:, None]) & (~rowpart)[:, None]
        rows.append(as_row | as_col)
        masks.append((blk[:, None] == blk[None, :]) & rowpart[:, None] & (~rowpart)[None, :])
        h //= 2
    fwd = np.concatenate(rows, axis=0).astype(np.float32)
    mask_f = np.stack(masks).astype(np.float32)
    n_blocks = fwd.shape[0] // n
    rev = fwd.reshape(n_blocks, n, n)[:, ::-1, ::-1].reshape(fwd.shape)
    mask_r = mask_f[:, ::-1, ::-1]
    return np.stack([fwd, rev]), np.stack([mask_f, mask_r])


def _gla_gates(inputs, lbp_ref, tab_ref, g_ref, e_ref):
    dk = g_ref.shape[3] // HGRN_HEADS
    for d, (q_ref, z_ref, v_ref) in enumerate(inputs):
        for h in range(HGRN_HEADS):
            sl = slice(h * dk, (h + 1) * dk)
            lbp = lbp_ref[d, :, sl]
            log_lb, log_1mlb, one_mlb = lbp[0:1], lbp[1:2], lbp[2:3]
            z = z_ref[:, sl]
            log_sig = jnp.minimum(z, 0.0) - jnp.log(1.0 + jnp.exp(-jnp.abs(z)))
            bv = log_1mlb + log_sig
            lf = jnp.maximum(log_lb, bv) + jnp.log(1.0 + jnp.exp(-jnp.abs(log_lb - bv)))
            g_ref[d, 0, :, sl] = _silu(q_ref[:, sl])
            g_ref[d, 1, :, sl] = one_mlb * _sigmoid(-z)
            g_ref[d, 2, :, sl] = v_ref[:, sl]
            e_ref[d, :, sl] = jnp.exp(_dot01(tab_ref[d], lf))
            yield


def _gla_scan(g_ref, e_ref, mask_ref, s_ref):
    c = CHUNK
    n_levels = mask_ref.shape[1] - 1
    bf = lambda t: t.astype(BF16)
    mm = lambda a, b: jnp.dot(a, b, preferred_element_type=F32)
    mm_nt = lambda a, b: lax.dot_general(a, b, (((1,), (1,)), ((), ())), preferred_element_type=F32)
    mm_tn = lambda a, b: lax.dot_general(a, b, (((0,), (0,)), ((), ())), preferred_element_type=F32)
    dk = g_ref.shape[3] // HGRN_HEADS
    chains = []
    for d in range(2):
        masks = mask_ref[d]
        for h in range(HGRN_HEADS):
            sl = slice(h * dk, (h + 1) * dk)
            qh, kh, ex = g_ref[d, 0, :, sl], g_ref[d, 1, :, sl], e_ref[d, :, sl]
            ch = dict(d=d, h=h, v=bf(g_ref[d, 2, :, sl]),
                      q_in=bf(qh * ex[0:c]), k_out=bf(kh * ex[c:2 * c]),
                      w_tot=ex[0:1] if d == 1 else ex[c - 1:c])
            att = masks[0] * mm_nt(bf(qh), bf(kh))
            for l in range(n_levels):
                lv = ex[(2 + l) * c:(3 + l) * c]
                att = att + masks[l + 1] * mm_nt(bf(qh * lv), bf(kh * lv))
            ch["att"] = bf(att)
            chains.append(ch)
            yield
    outs = [[None] * HGRN_HEADS for _ in range(2)]
    for ch in chains:
        st = s_ref[ch["d"], ch["h"]]
        outs[ch["d"]][ch["h"]] = mm(ch["att"], ch["v"]) + mm_nt(ch["q_in"], bf(st))
        s_ref[ch["d"], ch["h"]] = st * ch["w_tot"] + mm_tn(ch["v"], ch["k_out"])
    return [jnp.concatenate(o, axis=1) for o in outs]


def _shift_conv(prev_ref, cur_ref, nxt_ref, conv_ref, out_ref, is_lat, has_prev, has_next):
    n, width = cur_ref.shape
    row = lax.broadcasted_iota(jnp.int32, (n, LANES), 0)
    lat = jnp.where(is_lat, 1.0, 0.0)
    hp = jnp.where(has_prev, 1.0, 0.0)
    hn = jnp.where(has_next, 1.0, 0.0)
    edge_l = (1.0 - lat) * hp
    edge_r = (1.0 - lat) * hn
    for s in range(width // LANES):
        sl = slice(s * LANES, (s + 1) * LANES)
        kern = conv_ref[:, sl]
        prev, cur, nxt = prev_ref[:, sl], cur_ref[:, sl], nxt_ref[:, sl]
        k_up = kern[0:3] * (lat * hp)
        k_dn = kern[6:9] * (lat * hn)

        def column(j):
            return prev * k_up[j:j + 1] + cur * kern[3 + j:4 + j] + nxt * k_dn[j:j + 1]

        fill_l = prev[n - 1:n] * (edge_l * kern[3:4])
        fill_r = nxt[0:1] * (edge_r * kern[5:6])
        out_ref[:, sl] = (column(1)
                          + jnp.where(row == 0, fill_l, pltpu.roll(column(0), 1, 0))
                          + jnp.where(row == n - 1, fill_r, pltpu.roll(column(2), n - 1, 0)))
        if s == width // LANES - 1:
            yield


def _pair_diag(x):
    lo = lax.broadcasted_iota(jnp.int32, x.shape, 1) < RWKV_DH
    z = jnp.zeros_like(x)
    return jnp.concatenate([jnp.where(lo, x, z), jnp.where(lo, z, x)], axis=0)


def _rwkv_scan_chunks(q_ref, s_ref):
    c = CHUNK
    pw = 2 * RWKV_DH
    n_pairs = q_ref.shape[3] // pw
    row = lax.broadcasted_iota(jnp.int32, (c, pw), 0)
    lane = lax.broadcasted_iota(jnp.int32, (c, pw), 1)
    col = lane & (RWKV_DH - 1)
    lo = lane < RWKV_DH
    eye = row == col
    bf = lambda t: t.astype(BF16)
    mm = lambda a, b: jnp.dot(a, b, preferred_element_type=F32)
    mm_nt = lambda a, b: lax.dot_general(a, b, (((1,), (1,)), ((), ())), preferred_element_type=F32)
    mm_tn = lambda a, b: lax.dot_general(a, b, (((0,), (0,)), ((), ())), preferred_element_type=F32)

    chains = []
    for d in range(2):
        rev = d == 1
        tri = (row <= col) if rev else (row >= col)
        strict = (row < col) if rev else (row > col)
        tri01 = jnp.where(_tri(rev, False), 1.0, 0.0).astype(BF16)
        for j in range(n_pairs):
            sl = slice(j * pw, (j + 1) * pw)
            r, lw, k, v, kk, a = (q_ref[d, t, :, sl] for t in range(6))
            b_incl = _dot01(tri01, lw)
            b_tot = b_incl[0:1] if rev else b_incl[c - 1:c]
            e_in = jnp.exp(-b_incl)
            e_out = jnp.exp(b_tot - b_incl)
            beta = kk * a
            chains.append(dict(
                d=d, j=j, tri=tri, strict=strict, w_tot=jnp.exp(b_tot), v=v,
                ar=bf(jnp.concatenate([-kk * jnp.exp(b_incl - lw), r * jnp.exp(b_incl)], axis=0)),
                bt=_pair_diag(bf(beta * e_in)), kt=_pair_diag(bf(k * e_in)),
                bk=bf(jnp.concatenate([beta * e_out, k * e_out], axis=0))))
    yield
    for ch in chains:
        g_b = mm_nt(ch["ar"], ch["bt"])
        g_k = mm_nt(ch["ar"], ch["kt"])
        a_ab = jnp.where(ch["strict"], g_b[0:c], 0.0)
        ch["a_rb"] = bf(jnp.where(ch["tri"], g_b[c:], 0.0))
        ch["a_k"] = bf(jnp.concatenate([jnp.where(ch["strict"], g_k[0:c], 0.0),
                                        jnp.where(ch["tri"], g_k[c:], 0.0)], axis=0))
        ch["p"] = jnp.where(eye, 1.0, a_ab)
        ch["ak"] = bf(a_ab)
    yield
    for ch in chains:
        ch["ak2"] = mm(ch["ak"], _pair_diag(ch["ak"]))
    yield
    n_lv = int(math.log2(c))
    for lv in range(2, n_lv + 1):
        for ch in chains:
            ak = bf(ch["ak2"])
            p_bd = _pair_diag(bf(ch["p"]))
            if lv < n_lv:
                both = mm(ak, jnp.concatenate([_pair_diag(ak), p_bd], axis=1))
                ch["ak2"] = both[:, :pw]
                ch["p"] = ch["p"] + both[:, pw:]
            else:
                ch["p"] = ch["p"] + mm(ak, p_bd)
        yield
    for ch in chains:
        s0 = s_ref[ch["d"], ch["j"]]
        ch["s0"] = s0
        ch["xs"] = mm_nt(ch["ar"], _pair_diag(bf(s0)))
        ch["av"] = mm(ch["a_k"], _pair_diag(bf(ch["v"])))
    yield
    for ch in chains:
        x = ch["xs"][0:c] + ch["av"][0:c]
        ch["u"] = mm(bf(ch["p"]), _pair_diag(bf(x)))
    yield
    outs = [[None] * n_pairs for _ in range(2)]
    for ch in chains:
        u = ch["u"]
        outs[ch["d"]][ch["j"]] = ch["xs"][c:] + ch["av"][c:] + mm(ch["a_rb"], _pair_diag(bf(u)))
        z = mm_tn(bf(jnp.concatenate([u, ch["v"]], axis=0)), ch["bk"])
        s_ref[ch["d"], ch["j"]] = ch["s0"] * ch["w_tot"] + jnp.where(lo, z[0:c], z[c:])
    return [jnp.concatenate(o, axis=1) for o in outs]


def _rwkv_prep(rw_ref, vec_ref, wup_ref, aup_ref, gup_ref, bd_ref, q_ref, bonus_ref, gr_ref, d):
    dr = q_ref.shape[3]
    o_w = 3 * dr
    o_a = o_w + 2 * W_RANK
    o_g = o_a + 2 * A_RANK
    bd = bd_ref[0:LANES, 0:LANES]
    tanh_wd = jnp.tanh(rw_ref[:, o_w + d * W_RANK:o_w + (d + 1) * W_RANK]).astype(BF16)
    ad = [rw_ref[:, o_a + e * A_RANK:o_a + (e + 1) * A_RANK].astype(BF16) for e in range(2)]
    if d == 0:
        sig_gd = _sigmoid(rw_ref[:, o_g:o_g + G_RANK]).astype(BF16)
    mm = lambda a, b: jnp.dot(a, b, preferred_element_type=F32)
    for j in range(dr // LANES):
        sl = slice(j * LANES, (j + 1) * LANES)
        vec = vec_ref[:, sl]
        w0, a0 = vec[0:2], vec[2:4]
        k_k, k_a, r_k = vec[4:5], vec[5:6], vec[6:7]
        r = rw_ref[:, j * LANES:(j + 1) * LANES]
        k = rw_ref[:, dr + j * LANES:dr + (j + 1) * LANES]
        v = rw_ref[:, 2 * dr + j * LANES:2 * dr + (j + 1) * LANES]
        lw = -math.exp(-0.5) * _sigmoid(w0[d:d + 1] + mm(tanh_wd, wup_ref[d, :, sl]))
        a = _sigmoid(a0[d:d + 1] + mm(ad[d], aup_ref[d, :, sl]))
        kk = k * k_k
        kk = kk / jnp.maximum(jnp.sqrt(_dot_x01(kk * kk, bd)), 1e-12)
        for t, val in enumerate((r, lw, k * (1.0 + (a - 1.0) * k_a), v, kk, a)):
            q_ref[d, t, :, sl] = val
        if d == 0:
            a_b = _sigmoid(a0[1:2] + mm(ad[1], aup_ref[1, :, sl]))
            k_sum = k * (2.0 + (a + a_b - 2.0) * k_a)
            bonus_ref[:, sl] = _dot_x01(r * k_sum * r_k, bd) * v
            gr_ref[:, sl] = mm(sig_gd, gup_ref[:, sl])
        if j % 2 == 1:
            yield


def _mix_kernel(qf_ref, zf_ref, vf_ref, qb_ref, zb_ref, vb_ref, lbp_ref, tab_ref, mask_ref,
                rwf_ref, rwb_ref, vec_ref, wup_ref, aup_ref, gup_ref,
                bd_ref, of_ref, ob_ref, yf_ref, yb_ref, bonus_ref, gr_ref,
                hs_ref, g_ref, e_ref, s_ref, q_ref):
    step = pl.program_id(1)

    @pl.when(step == 0)
    def _():
        g_ref[...] = jnp.zeros(g_ref.shape, F32)
        e_ref[...] = jnp.zeros(e_ref.shape, F32)
        q_ref[...] = jnp.zeros(q_ref.shape, F32)

    @pl.when(step <= 1)
    def _():
        hs_ref[...] = jnp.zeros(hs_ref.shape, F32)
        s_ref[...] = jnp.zeros(s_ref.shape, F32)

    scan = _rwkv_scan_chunks(q_ref, s_ref)
    next(scan)

    def prepare():
        for d, rw_ref in enumerate((rwf_ref, rwb_ref)):
            yield from _rwkv_prep(rw_ref, vec_ref, wup_ref, aup_ref, gup_ref, bd_ref, q_ref, bonus_ref,
                                  gr_ref, d)

    (y_f, y_b), _, (o_f, o_b), _ = _interleave(
        scan, prepare(), _gla_scan(g_ref, e_ref, mask_ref, hs_ref),
        _gla_gates(((qf_ref, zf_ref, vf_ref), (qb_ref, zb_ref, vb_ref)), lbp_ref, tab_ref, g_ref, e_ref))
    yf_ref[...] = y_f
    yb_ref[...] = y_b
    of_ref[...] = o_f
    ob_ref[...] = o_b


def _mix_call(p_hg, p_rw, lbp, vec, w_up, a_up, g_up, bd, *, n_batch, nc, nl):
    rows, n_rw = p_rw.shape
    dh = p_hg.shape[1] // 5
    dr = bd.shape[0]
    nt = nc + nl
    rb, c_fwd, c_bwd = _chunk_maps(n_batch, nc, nl)
    tab, masks = _gla_tables()
    tab = jnp.asarray(tab, BF16)
    masks = jnp.asarray(masks, F32)
    nxt = lambda i: jnp.minimum(i, nt - 1)
    cur = lambda i: jnp.maximum(i - 1, 0)

    def hg(col, cmap):
        return pl.BlockSpec((CHUNK, dh), lambda b, i: (rb(b, cmap(nxt(i))), col))

    def rw(cmap):
        return pl.BlockSpec((CHUNK, n_rw), lambda b, i: (rb(b, cmap(nxt(i))), 0))

    def out(width, cmap, when):
        return pl.BlockSpec((CHUNK, width), lambda b, i: (rb(b, cmap(when(i))), 0))

    dk = dh // HGRN_HEADS
    n_heads = dr // RWKV_DH
    consts = (lbp, tab, masks)
    consts2 = (vec, w_up, a_up, g_up, bd)
    return pl.pallas_call(
        _mix_kernel,
        grid=(n_batch, nt + 1),
        in_specs=[hg(0, c_fwd), hg(1, c_fwd), hg(3, c_fwd), hg(0, c_bwd), hg(2, c_bwd), hg(3, c_bwd)]
        + [_resident(a.shape) for a in consts]
        + [rw(c_fwd), rw(c_bwd)]
        + [_resident(a.shape) for a in consts2],
        out_specs=[out(dh, c_fwd, cur), out(dh, c_bwd, cur), out(dr, c_fwd, cur), out(dr, c_bwd, cur),
                   out(dr, c_fwd, nxt), out(dr, c_fwd, nxt)],
        out_shape=[jax.ShapeDtypeStruct((rows, dh), F32)] * 2 + [jax.ShapeDtypeStruct((rows, dr), F32)] * 4,
        scratch_shapes=[pltpu.VMEM((2, HGRN_HEADS, dk, dk), F32),
                        pltpu.VMEM((2, 3, CHUNK, dh), F32),
                        pltpu.VMEM((2, tab.shape[1], dh), F32),
                        pltpu.VMEM((2, n_heads // 2, RWKV_DH, 2 * RWKV_DH), F32),
                        pltpu.VMEM((2, 6, CHUNK, dr), F32)],
        compiler_params=_cparams("arbitrary", "arbitrary"),
        name="token_mix_scans",
    )(p_hg, p_hg, p_hg, p_hg, p_hg, p_hg, *consts, p_rw, p_rw, *consts2)


def _post_kernel(x_ref, m_ref, of_ref, ob_ref, go_ref, yf_ref, yb_ref, bonus_ref, gr_ref, vec_ref,
                 bd_ref, wo_ref, o_ref):
    vec = vec_ref[...]
    o_gain, ln_w, ln_b = vec[0:1], vec[1:2], vec[2:3]
    o = of_ref[...] + ob_ref[...]
    dh = o.shape[1] // HGRN_HEADS
    parts = []
    for h in range(HGRN_HEADS):
        oh = o[:, h * dh:(h + 1) * dh]
        parts.append(oh * lax.rsqrt(jnp.mean(oh * oh, axis=-1, keepdims=True) + RMS_EPS))
    o = jnp.concatenate(parts, axis=1) * o_gain * _silu(go_ref[...])
    y = yf_ref[...] + yb_ref[...]
    bd = bd_ref[...]
    inv = 1.0 / RWKV_DH
    mu = _dot_x01(y, bd) * inv
    yc = y - mu
    var = _dot_x01(yc * yc, bd) * inv
    y = yc * lax.rsqrt(var + GN_EPS) * ln_w + ln_b
    y = (y + bonus_ref[...]) * gr_ref[...]
    mixed = jnp.concatenate([o, y], axis=1).astype(BF16)
    out = jnp.dot(mixed, wo_ref[...], preferred_element_type=F32)
    o_ref[...] = x_ref[...] + m_ref[0][5:6] * out


def _post_call(xa, mods, o_f, o_b, p_hg, y_f, y_b, bonus, g_r, vec, bd, w_out, *, tm, ctx_tiles,
               tiles_per_batch, n_batch):
    rows, d = xa.shape
    dh = o_f.shape[1]
    dr = y_f.shape[1]
    row = lambda w: pl.BlockSpec((tm, w), lambda i: (i, 0))
    return pl.pallas_call(
        _post_kernel,
        grid=(rows // tm,),
        in_specs=[row(d),
                  pl.BlockSpec((1, N_MOD, d),
                               lambda i: (_mod_index(i, ctx_tiles, tiles_per_batch, n_batch), 0, 0)),
                  row(dh), row(dh), pl.BlockSpec((tm, dh), lambda i: (i, 4)),
                  row(dr), row(dr), row(dr), row(dr),
                  _resident(vec.shape), _resident(bd.shape), _resident(w_out.shape)],
        out_specs=row(d),
        out_shape=jax.ShapeDtypeStruct((rows, d), F32),
        compiler_params=_cparams("arbitrary"),
        name="mix_out_projection",
    )(xa, mods, o_f, o_b, p_hg, y_f, y_b, bonus, g_r, vec, bd, w_out)


def kernel(x, c, ctx, c_ctx, w_mod, b_mod, norm_gains, final_gain, ffn1_up, ffn1_down, ffn2_up, ffn2_down, w_in, w_out, hgrn_lb_logits, hgrn_o_gain, rwkv_conv, rwkv_w0, rwkv_w_up, rwkv_a0, rwkv_a_up, rwkv_g_up, rwkv_k_k, rwkv_k_a, rwkv_r_k, rwkv_ln_w, rwkv_ln_b):
    n_batch, seq, d = x.shape
    n_ctx = ctx.shape[1]
    depth = w_mod.shape[0]
    d_ff = ffn1_down.shape[1]
    d_hgrn = hgrn_o_gain.shape[1]
    d_rwkv = rwkv_k_k.shape[1]
    hg_cols = 5 * d_hgrn
    assert seq % CHUNK == 0 and n_ctx % CHUNK == 0 and GRID_W == CHUNK and n_batch < 8
    nc, nl = n_ctx // CHUNK, seq // CHUNK

    p = jax.nn.softmax(hgrn_lb_logits.astype(F32), axis=0)
    cum = jnp.cumsum(p, axis=0)
    lb = cum - cum[0]
    lbp = jnp.stack([jnp.log(lb), jnp.log1p(-lb), 1.0 - lb], axis=2)

    cvec = jnp.concatenate([c, c_ctx[None], jnp.zeros((7 - n_batch, d), F32)], axis=0)
    mods = _mod_call(cvec, w_mod, b_mod).reshape(depth, 8, N_MOD, d)

    bd = jnp.asarray(np.kron(np.eye(d_rwkv // RWKV_DH), np.ones((RWKV_DH, RWKV_DH))), BF16)

    xa = jnp.concatenate([ctx.reshape(n_batch * n_ctx, d), x.reshape(n_batch * seq, d)], axis=0)
    tm = _row_tile(n_batch * n_ctx, seq, 1024)
    tm_s = _row_tile(n_batch * n_ctx, seq, 512)
    tiles = dict(tm=tm, ctx_tiles=n_batch * n_ctx // tm, tiles_per_batch=seq // tm, n_batch=n_batch)
    tiles_s = dict(tm=tm_s, ctx_tiles=n_batch * n_ctx // tm_s, tiles_per_batch=seq // tm_s, n_batch=n_batch)

    for l in range(depth):
        last = l == depth - 1
        bf = lambda w: w.astype(BF16)
        xa = _ffn_call(xa, mods[l], norm_gains[l, 0], bf(ffn1_up[l, :, :d_ff]), bf(ffn1_up[l, :, d_ff:]),
                       bf(ffn1_down[l]), mod_base=0, **tiles)
        p_hg, p_rw = _inproj_call(xa, mods[l], norm_gains[l, 1], bf(w_in[l, :, :hg_cols]),
                                  bf(w_in[l, :, hg_cols:]), rwkv_conv[l].reshape(9, -1), nc=nc, nl=nl,
                                  **tiles_s)
        vec = jnp.concatenate([rwkv_w0[l], rwkv_a0[l], rwkv_k_k[l][None], rwkv_k_a[l][None],
                               rwkv_r_k[l].reshape(1, d_rwkv), jnp.zeros((1, d_rwkv), F32)], axis=0)
        o_f, o_b, y_f, y_b, bonus, g_r = _mix_call(
            p_hg, p_rw, lbp[l], vec, bf(rwkv_w_up[l]), bf(rwkv_a_up[l]), bf(rwkv_g_up[l]), bd,
            n_batch=n_batch, nc=nc, nl=nl)
        vec2 = jnp.concatenate([hgrn_o_gain[l][None], rwkv_ln_w[l][None], rwkv_ln_b[l][None],
                                jnp.zeros((5, d_rwkv), F32)], axis=0)
        xa = _post_call(xa, mods[l], o_f, o_b, p_hg, y_f, y_b, bonus, g_r, vec2, bd, bf(w_out[l]),
                        **tiles_s)
        up, down = bf(ffn2_up[l]), bf(ffn2_down[l])
        xa = _ffn_call(xa, mods[l], norm_gains[l, 2], up[:, :d_ff], up[:, d_ff:], down, mod_base=6,
                       skip_ctx=last, final_gain=final_gain if last else None, **tiles)
    return xa.reshape(n_batch, seq, d)
```

```python
import functools
import math

import numpy as np
import jax
import jax.numpy as jnp
from jax import lax
from jax.experimental import pallas as pl
from jax.experimental.pallas import tpu as pltpu

F32 = jnp.float32
BF16 = jnp.bfloat16

CHUNK = 64
GRID_W = 64
HGRN_HEADS = 4
RWKV_DH = 64
LANES = 128
W_RANK = 64
A_RANK = 64
G_RANK = 128
N_MOD = 9
RMS_EPS = 1e-6
GN_EPS = 64e-5
VMEM_LIMIT = 56 * 1024 * 1024


def _cparams(*sem):
    return pltpu.CompilerParams(dimension_semantics=sem, vmem_limit_bytes=VMEM_LIMIT)


def _dot(a, b):
    return jnp.dot(a.astype(BF16), b.astype(BF16), preferred_element_type=F32)


def _dot_nt(a, b):
    return lax.dot_general(a.astype(BF16), b.astype(BF16), (((1,), (1,)), ((), ())),
                           preferred_element_type=F32)


def _dot_tn(a, b):
    return lax.dot_general(a.astype(BF16), b.astype(BF16), (((0,), (0,)), ((), ())),
                           preferred_element_type=F32)


def _split2(x):
    hi = x.astype(BF16)
    return hi, (x - hi.astype(F32)).astype(BF16)


def _dot01(m01, x):
    hi, lo = _split2(x)
    return (jnp.dot(m01, hi, preferred_element_type=F32)
            + jnp.dot(m01, lo, preferred_element_type=F32))


def _dot_x01(x, m01):
    hi, lo = _split2(x)
    return (jnp.dot(hi, m01, preferred_element_type=F32)
            + jnp.dot(lo, m01, preferred_element_type=F32))


def _sigmoid(x):
    return 0.5 * jnp.tanh(0.5 * x) + 0.5


def _silu(x):
    return x * _sigmoid(x)


def _rms_mod(x, gain, shift, scale):
    y = x * lax.rsqrt(jnp.mean(x * x, axis=-1, keepdims=True) + RMS_EPS)
    return (y * gain) * (1.0 + scale) + shift


def _mod_kernel(c_ref, w_ref, b_ref, o_ref):
    s = _silu(c_ref[...])
    o_ref[0] = jnp.dot(s, w_ref[0], preferred_element_type=F32,
                       precision=lax.Precision.HIGHEST) + b_ref[0]


def _mod_call(cvec, w_mod, b_mod):
    depth, d, n = w_mod.shape
    tn = 1152 if n % 1152 == 0 else n
    return pl.pallas_call(
        _mod_kernel,
        grid=(depth, n // tn),
        in_specs=[pl.BlockSpec((8, d), lambda l, j: (0, 0)),
                  pl.BlockSpec((1, d, tn), lambda l, j: (l, 0, j)),
                  pl.BlockSpec((1, 1, tn), lambda l, j: (l, 0, j))],
        out_specs=pl.BlockSpec((1, 8, tn), lambda l, j: (l, 0, j)),
        out_shape=jax.ShapeDtypeStruct((depth, 8, n), F32),
        compiler_params=_cparams("arbitrary", "arbitrary"),
        name="mod_vectors",
    )(cvec, w_mod, b_mod.reshape(depth, 1, n))


def _row_tile(n_ctx_rows, seq, cap):
    tm = cap
    while n_ctx_rows % tm or seq % tm:
        tm //= 2
    return tm


def _mod_index(i, ctx_tiles, tiles_per_batch, n_batch):
    return jnp.where(i < ctx_tiles, n_batch, (i - ctx_tiles) // tiles_per_batch)


def _resident(shape):
    return pl.BlockSpec(shape, lambda *_: (0,) * len(shape), pipeline_mode=pl.Buffered(1))


def _ffn_kernel(*refs, mod_base, ff_chunk, final, split_at):
    if split_at:
        c_ref, x_ref, *refs = refs
        x = jnp.where(pl.program_id(0) < split_at, c_ref[...], x_ref[...])
    else:
        x_ref, *refs = refs
        x = x_ref[...]
    m_ref, g_ref, wg_ref, wu_ref, wd_ref, *rest = refs
    if final:
        fg_ref, o_ref = rest
    else:
        (o_ref,) = rest
    m = m_ref[0]
    h = _rms_mod(x, g_ref[...], m[mod_base:mod_base + 1], m[mod_base + 1:mod_base + 2]).astype(BF16)
    d_ff = wd_ref.shape[0]
    acc = jnp.zeros(x.shape, F32)
    for j in range(d_ff // ff_chunk):
        sl = slice(j * ff_chunk, (j + 1) * ff_chunk)
        gate = jnp.dot(h, wg_ref[:, sl], preferred_element_type=F32)
        up = jnp.dot(h, wu_ref[:, sl], preferred_element_type=F32)
        act = (_silu(gate) * up).astype(BF16)
        acc = acc + jnp.dot(act, wd_ref[sl, :], preferred_element_type=F32)
    y = x + (0.5 * m[mod_base + 2:mod_base + 3]) * acc
    if final:
        y = (y * lax.rsqrt(jnp.mean(y * y, axis=-1, keepdims=True) + RMS_EPS)) * fg_ref[...]
    o_ref[...] = y


def _ffn_call(xa, mods, gain, wg, wu, wd, *, mod_base, tm, ctx_tiles, tiles_per_batch, n_batch,
              skip_ctx=False, final_gain=None):
    split = isinstance(xa, tuple)
    d = wd.shape[1]
    rows = sum(a.shape[0] for a in xa) if split else xa.shape[0]
    d_ff = wd.shape[0]
    n_tiles = rows // tm
    t0 = ctx_tiles if skip_ctx else 0
    final = final_gain is not None
    if split:
        x_specs = [pl.BlockSpec((tm, d), lambda i: (jnp.minimum(i, ctx_tiles - 1), 0)),
                   pl.BlockSpec((tm, d), lambda i: (jnp.maximum(i - ctx_tiles, 0), 0))]
    else:
        x_specs = [pl.BlockSpec((tm, d), lambda i: (i + t0, 0))]
    in_specs = x_specs + [
        pl.BlockSpec((1, N_MOD, d),
                     lambda i: (_mod_index(i + t0, ctx_tiles, tiles_per_batch, n_batch), 0, 0)),
        _resident((1, d)), _resident((d, d_ff)), _resident((d, d_ff)), _resident((d_ff, d))]
    args = [*(xa if split else (xa,)), mods, gain.reshape(1, d), wg, wu, wd]
    if final:
        in_specs.append(_resident((1, d)))
        args.append(final_gain.reshape(1, d))
    out_rows = rows - t0 * tm
    return pl.pallas_call(
        functools.partial(_ffn_kernel, mod_base=mod_base, ff_chunk=256, final=final,
                          split_at=ctx_tiles if split else 0),
        grid=(n_tiles - t0,),
        in_specs=in_specs,
        out_specs=pl.BlockSpec((tm, d), lambda i: (i, 0)),
        out_shape=jax.ShapeDtypeStruct((out_rows, d), F32),
        compiler_params=_cparams("arbitrary"),
        name="ffn_half_step",
    )(*args)


def _inproj_kernel(x_ref, m_ref, g_ref, whg_ref, wrw_ref, conv_ref, ohg_ref, orw_ref, tile_ref, edge_ref,
                   *, ctx_chunks, nc, nl):
    step = pl.program_id(0)
    tm = x_ref.shape[0]
    cpt = tm // CHUNK
    new = tile_ref.at[step % 2]
    old = tile_ref.at[(step + 1) % 2]

    @pl.when(step == 0)
    def _():
        tile_ref[...] = jnp.zeros(tile_ref.shape, F32)
        edge_ref[...] = jnp.zeros(edge_ref.shape, F32)

    m = m_ref[0]
    h = _rms_mod(x_ref[...], g_ref[...], m[3:4], m[4:5]).astype(BF16)

    def project():
        for w_ref, o_ref, n_chunks in ((wrw_ref, new, 5), (whg_ref, ohg_ref, 5)):
            width = w_ref.shape[1] // n_chunks
            for j in range(n_chunks):
                sl = slice(j * width, (j + 1) * width)
                o_ref[:, sl] = jnp.dot(h, w_ref[:, sl], preferred_element_type=F32)
                yield

    def shift():
        first = jnp.maximum(step - 1, 0) * cpt
        yield
        for c in range(cpt):
            g = first + c
            is_lat = g >= ctx_chunks
            pos = jnp.where(is_lat, (g - ctx_chunks) % nl, g % nc)
            has_prev = pos != 0
            has_next = pos != jnp.where(is_lat, nl - 1, nc - 1)
            prev = old.at[(c - 1) * CHUNK:c * CHUNK] if c else edge_ref
            nxt = old.at[(c + 1) * CHUNK:(c + 2) * CHUNK] if c < cpt - 1 else new.at[0:CHUNK]
            yield from _shift_conv(prev, old.at[c * CHUNK:(c + 1) * CHUNK], nxt, conv_ref,
                                   orw_ref.at[c * CHUNK:(c + 1) * CHUNK], is_lat, has_prev, has_next)

    _interleave(project(), shift())
    edge_ref[...] = old[tm - CHUNK:tm]


def _inproj_call(xa, mods, gain, w_hg, w_rw, conv, *, tm, ctx_tiles, tiles_per_batch, n_batch, nc, nl):
    rows, d = xa.shape
    n_hg, n_rw = w_hg.shape[1], w_rw.shape[1]
    last = rows // tm - 1
    cur = lambda i: jnp.minimum(i, last)
    return pl.pallas_call(
        functools.partial(_inproj_kernel, ctx_chunks=n_batch * nc, nc=nc, nl=nl),
        grid=(rows // tm + 1,),
        in_specs=[pl.BlockSpec((tm, d), lambda i: (cur(i), 0)),
                  pl.BlockSpec((1, N_MOD, d),
                               lambda i: (_mod_index(cur(i), ctx_tiles, tiles_per_batch, n_batch), 0, 0)),
                  _resident((1, d)), _resident((d, n_hg)), _resident((d, n_rw)), _resident(conv.shape)],
        out_specs=[pl.BlockSpec((tm, n_hg), lambda i: (cur(i), 0)),
                   pl.BlockSpec((tm, n_rw), lambda i: (jnp.maximum(i - 1, 0), 0))],
        out_shape=[jax.ShapeDtypeStruct((rows, n_hg), F32), jax.ShapeDtypeStruct((rows, n_rw), F32)],
        scratch_shapes=[pltpu.VMEM((2, tm, n_rw), F32), pltpu.VMEM((CHUNK, n_rw), F32)],
        compiler_params=_cparams("arbitrary"),
        name="in_projection",
    )(xa, mods, gain.reshape(1, d), w_hg, w_rw, conv)


def _chunk_maps(n_batch, nc, nl):
    nt = nc + nl

    def rb(b, c):
        return jnp.where(c < nc, b * nc + c, n_batch * nc + b * nl + (c - nc))

    def c_fwd(i):
        return i

    def c_bwd(i):
        return jnp.where(i < nc, nc - 1 - i, nt + nc - 1 - i)

    return rb, c_fwd, c_bwd


def _interleave(*stages):
    values = [None] * len(stages)
    live = list(range(len(stages)))
    while live:
        for g in tuple(live):
            try:
                next(stages[g])
            except StopIteration as stop:
                values[g] = stop.value
                live.remove(g)
    return values


def _tri(rev, strict):
    r = lax.broadcasted_iota(jnp.int32, (CHUNK, CHUNK), 0)
    c = lax.broadcasted_iota(jnp.int32, (CHUNK, CHUNK), 1)
    if rev:
        return (r < c) if strict else (r <= c)
    return (r > c) if strict else (r >= c)


def _gla_tables():
    n = CHUNK
    idx = np.arange(n)
    incl = (idx[:, None] >= idx[None, :])
    rows = [incl, (idx[None, :] > idx[:, None])]
    masks = [np.eye(n, dtype=bool)]
    h = n // 2
    while h >= 1:
        blk = idx // (2 * h)
        mid = blk * 2 * h + h
        rowpart = (idx % (2 * h)) >= h
        as_row = (idx[None, :] > mid[:, None]) & (idx[None, :] <= idx[:, None]) & rowpart[:, None]
        as_col = (idx[None, :] > idx[:, None]) & (idx[None, :] <= mid[:, None]) & (~rowpart)[:, None]
        rows.append(as_row | as_col)
        masks.append((blk[:, None] == blk[None, :]) & rowpart[:, None] & (~rowpart)[None, :])
        h //= 2
    fwd = np.concatenate(rows, axis=0).astype(np.float32)
    mask_f = np.stack(masks).astype(np.float32)
    n_blocks = fwd.shape[0] // n
    rev = fwd.reshape(n_blocks, n, n)[:, ::-1, ::-1].reshape(fwd.shape)
    mask_r = mask_f[:, ::-1, ::-1]
    return np.stack([fwd, rev]), np.stack([mask_f, mask_r])


def _gla_gates(inputs, lbp_ref, tab_ref, g_ref, e_ref):
    dk = g_ref.shape[3] // HGRN_HEADS
    for d, (q_ref, z_ref, v_ref) in enumerate(inputs):
        for h in range(HGRN_HEADS):
            sl = slice(h * dk, (h + 1) * dk)
            lbp = lbp_ref[d, :, sl]
            log_lb, log_1mlb, one_mlb = lbp[0:1], lbp[1:2], lbp[2:3]
            z = z_ref[:, sl]
            log_sig = jnp.minimum(z, 0.0) - jnp.log(1.0 + jnp.exp(-jnp.abs(z)))
            bv = log_1mlb + log_sig
            lf = jnp.maximum(log_lb, bv) + jnp.log(1.0 + jnp.exp(-jnp.abs(log_lb - bv)))
            g_ref[d, 0, :, sl] = _silu(q_ref[:, sl])
            g_ref[d, 1, :, sl] = one_mlb * _sigmoid(-z)
            g_ref[d, 2, :, sl] = v_ref[:, sl]
            e_ref[d, :, sl] = jnp.exp(_dot01(tab_ref[d], lf))
            yield


def _gla_scan(g_ref, e_ref, mask_ref, s_ref):
    c = CHUNK
    n_levels = mask_ref.shape[1] - 1
    bf = lambda t: t.astype(BF16)
    mm = lambda a, b: jnp.dot(a, b, preferred_element_type=F32)
    mm_nt = lambda a, b: lax.dot_general(a, b, (((1,), (1,)), ((), ())), preferred_element_type=F32)
    mm_tn = lambda a, b: lax.dot_general(a, b, (((0,), (0,)), ((), ())), preferred_element_type=F32)
    dk = g_ref.shape[3] // HGRN_HEADS
    chains = []
    for d in range(2):
        masks = mask_ref[d]
        for h in range(HGRN_HEADS):
            sl = slice(h * dk, (h + 1) * dk)
            qh, kh, ex = g_ref[d, 0, :, sl], g_ref[d, 1, :, sl], e_ref[d, :, sl]
            ch = dict(d=d, h=h, v=bf(g_ref[d, 2, :, sl]),
                      q_in=bf(qh * ex[0:c]), k_out=bf(kh * ex[c:2 * c]),
                      w_tot=ex[0:1] if d == 1 else ex[c - 1:c])
            att = masks[0] * mm_nt(bf(qh), bf(kh))
            for l in range(n_levels):
                lv = ex[(2 + l) * c:(3 + l) * c]
                att = att + masks[l + 1] * mm_nt(bf(qh * lv), bf(kh * lv))
            ch["att"] = bf(att)
            chains.append(ch)
            yield
    outs = [[None] * HGRN_HEADS for _ in range(2)]
    for ch in chains:
        st = s_ref[ch["d"], ch["h"]]
        outs[ch["d"]][ch["h"]] = mm(ch["att"], ch["v"]) + mm_nt(ch["q_in"], bf(st))
        s_ref[ch["d"], ch["h"]] = st * ch["w_tot"] + mm_tn(ch["v"], ch["k_out"])
    return [jnp.concatenate(o, axis=1) for o in outs]


def _shift_conv(prev_ref, cur_ref, nxt_ref, conv_ref, out_ref, is_lat, has_prev, has_next):
    n, width = cur_ref.shape
    row = lax.broadcasted_iota(jnp.int32, (n, LANES), 0)
    lat = jnp.where(is_lat, 1.0, 0.0)
    hp = jnp.where(has_prev, 1.0, 0.0)
    hn = jnp.where(has_next, 1.0, 0.0)
    edge_l = (1.0 - lat) * hp
    edge_r = (1.0 - lat) * hn
    for s in range(width // LANES):
        sl = slice(s * LANES, (s + 1) * LANES)
        kern = conv_ref[:, sl]
        prev, cur, nxt = prev_ref[:, sl], cur_ref[:, sl], nxt_ref[:, sl]
        k_up = kern[0:3] * (lat * hp)
        k_dn = kern[6:9] * (lat * hn)

        def column(j):
            return prev * k_up[j:j + 1] + cur * kern[3 + j:4 + j] + nxt * k_dn[j:j + 1]

        fill_l = prev[n - 1:n] * (edge_l * kern[3:4])
        fill_r = nxt[0:1] * (edge_r * kern[5:6])
        out_ref[:, sl] = (column(1)
                          + jnp.where(row == 0, fill_l, pltpu.roll(column(0), 1, 0))
                          + jnp.where(row == n - 1, fill_r, pltpu.roll(column(2), n - 1, 0)))
        if s == width // LANES - 1:
            yield


def _pair_diag(x):
    lo = lax.broadcasted_iota(jnp.int32, x.shape, 1) < RWKV_DH
    z = jnp.zeros_like(x)
    return jnp.concatenate([jnp.where(lo, x, z), jnp.where(lo, z, x)], axis=0)


def _rwkv_scan_chunks(q_ref, s_ref):
    c = CHUNK
    pw = 2 * RWKV_DH
    n_pairs = q_ref.shape[3] // pw
    row = lax.broadcasted_iota(jnp.int32, (c, pw), 0)
    lane = lax.broadcasted_iota(jnp.int32, (c, pw), 1)
    col = lane & (RWKV_DH - 1)
    lo = lane < RWKV_DH
    eye = row == col
    bf = lambda t: t.astype(BF16)
    mm = lambda a, b: jnp.dot(a, b, preferred_element_type=F32)
    mm_nt = lambda a, b: lax.dot_general(a, b, (((1,), (1,)), ((), ())), preferred_element_type=F32)
    mm_tn = lambda a, b: lax.dot_general(a, b, (((0,), (0,)), ((), ())), preferred_element_type=F32)

    chains = []
    for d in range(2):
        rev = d == 1
        tri = (row <= col) if rev else (row >= col)
        strict = (row < col) if rev else (row > col)
        tri01 = jnp.where(_tri(rev, False), 1.0, 0.0).astype(BF16)
        for j in range(n_pairs):
            sl = slice(j * pw, (j + 1) * pw)
            r, lw, k, v, kk, a = (q_ref[d, t, :, sl] for t in range(6))
            b_incl = _dot01(tri01, lw)
            b_tot = b_incl[0:1] if rev else b_incl[c - 1:c]
            e_in = jnp.exp(-b_incl)
            e_out = jnp.exp(b_tot - b_incl)
            beta = kk * a
            chains.append(dict(
                d=d, j=j, tri=tri, strict=strict, w_tot=jnp.exp(b_tot), v=v,
                ar=bf(jnp.concatenate([-kk * jnp.exp(b_incl - lw), r * jnp.exp(b_incl)], axis=0)),
                bt=_pair_diag(bf(beta * e_in)), kt=_pair_diag(bf(k * e_in)),
                bk=bf(jnp.concatenate([beta * e_out, k * e_out], axis=0))))
    yield
    for ch in chains:
        g_b = mm_nt(ch["ar"], ch["bt"])
        g_k = mm_nt(ch["ar"], ch["kt"])
        a_ab = jnp.where(ch["strict"], g_b[0:c], 0.0)
        ch["a_rb"] = bf(jnp.where(ch["tri"], g_b[c:], 0.0))
        ch["a_k"] = bf(jnp.concatenate([jnp.where(ch["strict"], g_k[0:c], 0.0),
                                        jnp.where(ch["tri"], g_k[c:], 0.0)], axis=0))
        ch["p"] = jnp.where(eye, 1.0, a_ab)
        ch["ak"] = bf(a_ab)
    yield
    for ch in chains:
        ch["ak2"] = mm(ch["ak"], _pair_diag(ch["ak"]))
    yield
    n_lv = int(math.log2(c))
    for lv in range(2, n_lv + 1):
        for ch in chains:
            ak = bf(ch["ak2"])
            p_bd = _pair_diag(bf(ch["p"]))
            if lv < n_lv:
                both = mm(ak, jnp.concatenate([_pair_diag(ak), p_bd], axis=1))
                ch["ak2"] = both[:, :pw]
                ch["p"] = ch["p"] + both[:, pw:]
            else:
                ch["p"] = ch["p"] + mm(ak, p_bd)
        yield
    for ch in chains:
        s0 = s_ref[ch["d"], ch["j"]]
        ch["s0"] = s0
        ch["xs"] = mm_nt(ch["ar"], _pair_diag(bf(s0)))
        ch["av"] = mm(ch["a_k"], _pair_diag(bf(ch["v"])))
    yield
    for ch in chains:
        x = ch["xs"][0:c] + ch["av"][0:c]
        ch["u"] = mm(bf(ch["p"]), _pair_diag(bf(x)))
    yield
    outs = [[None] * n_pairs for _ in range(2)]
    for ch in chains:
        u = ch["u"]
        outs[ch["d"]][ch["j"]] = ch["xs"][c:] + ch["av"][c:] + mm(ch["a_rb"], _pair_diag(bf(u)))
        z = mm_tn(bf(jnp.concatenate([u, ch["v"]], axis=0)), ch["bk"])
        s_ref[ch["d"], ch["j"]] = ch["s0"] * ch["w_tot"] + jnp.where(lo, z[0:c], z[c:])
    return [jnp.concatenate(o, axis=1) for o in outs]


def _rwkv_prep(rw_ref, vec_ref, wup_ref, aup_ref, gup_ref, bd_ref, q_ref, bonus_ref, gr_ref, d):
    dr = q_ref.shape[3]
    o_w = 3 * dr
    o_a = o_w + 2 * W_RANK
    o_g = o_a + 2 * A_RANK
    bd = bd_ref[0:LANES, 0:LANES]
    tanh_wd = jnp.tanh(rw_ref[:, o_w + d * W_RANK:o_w + (d + 1) * W_RANK]).astype(BF16)
    ad = [rw_ref[:, o_a + e * A_RANK:o_a + (e + 1) * A_RANK].astype(BF16) for e in range(2)]
    if d == 0:
        sig_gd = _sigmoid(rw_ref[:, o_g:o_g + G_RANK]).astype(BF16)
    mm = lambda a, b: jnp.dot(a, b, preferred_element_type=F32)
    for j in range(dr // LANES):
        sl = slice(j * LANES, (j + 1) * LANES)
        vec = vec_ref[:, sl]
        w0, a0 = vec[0:2], vec[2:4]
        k_k, k_a, r_k = vec[4:5], vec[5:6], vec[6:7]
        r = rw_ref[:, j * LANES:(j + 1) * LANES]
        k = rw_ref[:, dr + j * LANES:dr + (j + 1) * LANES]
        v = rw_ref[:, 2 * dr + j * LANES:2 * dr + (j + 1) * LANES]
        lw = -math.exp(-0.5) * _sigmoid(w0[d:d + 1] + mm(tanh_wd, wup_ref[d, :, sl]))
        a = _sigmoid(a0[d:d + 1] + mm(ad[d], aup_ref[d, :, sl]))
        kk = k * k_k
        kk = kk / jnp.maximum(jnp.sqrt(_dot_x01(kk * kk, bd)), 1e-12)
        for t, val in enumerate((r, lw, k * (1.0 + (a - 1.0) * k_a), v, kk, a)):
            q_ref[d, t, :, sl] = val
        if d == 0:
            a_b = _sigmoid(a0[1:2] + mm(ad[1], aup_ref[1, :, sl]))
            k_sum = k * (2.0 + (a + a_b - 2.0) * k_a)
            bonus_ref[:, sl] = (_dot_x01(r * k_sum * r_k, bd) * v).astype(bonus_ref.dtype)
            gr_ref[:, sl] = mm(sig_gd, gup_ref[:, sl]).astype(gr_ref.dtype)
        if j % 2 == 1:
            yield


def _mix_kernel(qf_ref, zf_ref, vf_ref, qb_ref, zb_ref, vb_ref, lbp_ref, tab_ref, mask_ref,
                rwf_ref, rwb_ref, vec_ref, wup_ref, aup_ref, gup_ref,
                bd_ref, of_ref, ob_ref, yf_ref, yb_ref, bonus_ref, gr_ref,
                hs_ref, g_ref, e_ref, s_ref, q_ref):
    step = pl.program_id(1)

    @pl.when(step == 0)
    def _():
        g_ref[...] = jnp.zeros(g_ref.shape, F32)
        e_ref[...] = jnp.zeros(e_ref.shape, F32)
        q_ref[...] = jnp.zeros(q_ref.shape, F32)

    @pl.when(step <= 1)
    def _():
        hs_ref[...] = jnp.zeros(hs_ref.shape, F32)
        s_ref[...] = jnp.zeros(s_ref.shape, F32)

    scan = _rwkv_scan_chunks(q_ref, s_ref)
    next(scan)

    def prepare():
        for d, rw_ref in enumerate((rwf_ref, rwb_ref)):
            yield from _rwkv_prep(rw_ref, vec_ref, wup_ref, aup_ref, gup_ref, bd_ref, q_ref, bonus_ref,
                                  gr_ref, d)

    (y_f, y_b), _, (o_f, o_b), _ = _interleave(
        scan, prepare(), _gla_scan(g_ref, e_ref, mask_ref, hs_ref),
        _gla_gates(((qf_ref, zf_ref, vf_ref), (qb_ref, zb_ref, vb_ref)), lbp_ref, tab_ref, g_ref, e_ref))
    for ref, val in ((yf_ref, y_f), (yb_ref, y_b), (of_ref, o_f), (ob_ref, o_b)):
        ref[...] = val.astype(ref.dtype)


def _mix_call(p_hg, p_rw, lbp, vec, w_up, a_up, g_up, bd, *, n_batch, nc, nl):
    rows, n_rw = p_rw.shape
    dh = p_hg.shape[1] // 5
    dr = bd.shape[0]
    nt = nc + nl
    rb, c_fwd, c_bwd = _chunk_maps(n_batch, nc, nl)
    tab, masks = _gla_tables()
    tab = jnp.asarray(tab, BF16)
    masks = jnp.asarray(masks, F32)
    nxt = lambda i: jnp.minimum(i, nt - 1)
    cur = lambda i: jnp.maximum(i - 1, 0)

    def hg(col, cmap):
        return pl.BlockSpec((CHUNK, dh), lambda b, i: (rb(b, cmap(nxt(i))), col))

    def rw(cmap):
        return pl.BlockSpec((CHUNK, n_rw), lambda b, i: (rb(b, cmap(nxt(i))), 0))

    def out(width, cmap, when):
        return pl.BlockSpec((CHUNK, width), lambda b, i: (rb(b, cmap(when(i))), 0))

    dk = dh // HGRN_HEADS
    n_heads = dr // RWKV_DH
    consts = (lbp, tab, masks)
    consts2 = (vec, w_up, a_up, g_up, bd)
    return pl.pallas_call(
        _mix_kernel,
        grid=(n_batch, nt + 1),
        in_specs=[hg(0, c_fwd), hg(1, c_fwd), hg(3, c_fwd), hg(0, c_bwd), hg(2, c_bwd), hg(3, c_bwd)]
        + [_resident(a.shape) for a in consts]
        + [rw(c_fwd), rw(c_bwd)]
        + [_resident(a.shape) for a in consts2],
        out_specs=[out(dh, c_fwd, cur), out(dh, c_bwd, cur), out(dr, c_fwd, cur), out(dr, c_bwd, cur),
                   out(dr, c_fwd, nxt), out(dr, c_fwd, nxt)],
        out_shape=[jax.ShapeDtypeStruct((rows, dh), BF16)] * 2 + [jax.ShapeDtypeStruct((rows, dr), BF16)] * 4,
        scratch_shapes=[pltpu.VMEM((2, HGRN_HEADS, dk, dk), F32),
                        pltpu.VMEM((2, 3, CHUNK, dh), F32),
                        pltpu.VMEM((2, tab.shape[1], dh), F32),
                        pltpu.VMEM((2, n_heads // 2, RWKV_DH, 2 * RWKV_DH), F32),
                        pltpu.VMEM((2, 6, CHUNK, dr), F32)],
        compiler_params=_cparams("arbitrary", "arbitrary"),
        name="token_mix_scans",
    )(p_hg, p_hg, p_hg, p_hg, p_hg, p_hg, *consts, p_rw, p_rw, *consts2)


def _post_kernel(x_ref, m_ref, of_ref, ob_ref, go_ref, yf_ref, yb_ref, bonus_ref, gr_ref, vec_ref,
                 bd_ref, wo_ref, o_ref):
    vec = vec_ref[...]
    o_gain, ln_w, ln_b = vec[0:1], vec[1:2], vec[2:3]
    f32 = lambda ref: ref[...].astype(F32)
    o = f32(of_ref) + f32(ob_ref)
    dh = o.shape[1] // HGRN_HEADS
    parts = []
    for h in range(HGRN_HEADS):
        oh = o[:, h * dh:(h + 1) * dh]
        parts.append(oh * lax.rsqrt(jnp.mean(oh * oh, axis=-1, keepdims=True) + RMS_EPS))
    o = jnp.concatenate(parts, axis=1) * o_gain * _silu(go_ref[...])
    y = f32(yf_ref) + f32(yb_ref)
    bd = bd_ref[...]
    inv = 1.0 / RWKV_DH
    mu = _dot_x01(y, bd) * inv
    yc = y - mu
    var = _dot_x01(yc * yc, bd) * inv
    y = yc * lax.rsqrt(var + GN_EPS) * ln_w + ln_b
    y = (y + f32(bonus_ref)) * f32(gr_ref)
    mixed = jnp.concatenate([o, y], axis=1).astype(BF16)
    out = jnp.dot(mixed, wo_ref[...], preferred_element_type=F32)
    o_ref[...] = x_ref[...] + m_ref[0][5:6] * out


def _post_call(xa, mods, o_f, o_b, p_hg, y_f, y_b, bonus, g_r, vec, bd, w_out, *, tm, ctx_tiles,
               tiles_per_batch, n_batch):
    rows, d = xa.shape
    dh = o_f.shape[1]
    dr = y_f.shape[1]
    row = lambda w: pl.BlockSpec((tm, w), lambda i: (i, 0))
    return pl.pallas_call(
        _post_kernel,
        grid=(rows // tm,),
        in_specs=[row(d),
                  pl.BlockSpec((1, N_MOD, d),
                               lambda i: (_mod_index(i, ctx_tiles, tiles_per_batch, n_batch), 0, 0)),
                  row(dh), row(dh), pl.BlockSpec((tm, dh), lambda i: (i, 4)),
                  row(dr), row(dr), row(dr), row(dr),
                  _resident(vec.shape), _resident(bd.shape), _resident(w_out.shape)],
        out_specs=row(d),
        out_shape=jax.ShapeDtypeStruct((rows, d), F32),
        compiler_params=_cparams("arbitrary"),
        name="mix_out_projection",
    )(xa, mods, o_f, o_b, p_hg, y_f, y_b, bonus, g_r, vec, bd, w_out)


def kernel(x, c, ctx, c_ctx, w_mod, b_mod, norm_gains, final_gain, ffn1_up, ffn1_down, ffn2_up, ffn2_down, w_in, w_out, hgrn_lb_logits, hgrn_o_gain, rwkv_conv, rwkv_w0, rwkv_w_up, rwkv_a0, rwkv_a_up, rwkv_g_up, rwkv_k_k, rwkv_k_a, rwkv_r_k, rwkv_ln_w, rwkv_ln_b):
    n_batch, seq, d = x.shape
    n_ctx = ctx.shape[1]
    depth = w_mod.shape[0]
    d_ff = ffn1_down.shape[1]
    d_hgrn = hgrn_o_gain.shape[1]
    d_rwkv = rwkv_k_k.shape[1]
    hg_cols = 5 * d_hgrn
    assert seq % CHUNK == 0 and n_ctx % CHUNK == 0 and GRID_W == CHUNK and n_batch < 8
    nc, nl = n_ctx // CHUNK, seq // CHUNK

    p = jax.nn.softmax(hgrn_lb_logits.astype(F32), axis=0)
    cum = jnp.cumsum(p, axis=0)
    lb = cum - cum[0]
    lbp = jnp.stack([jnp.log(lb), jnp.log1p(-lb), 1.0 - lb], axis=2)

    cvec = jnp.concatenate([c, c_ctx[None], jnp.zeros((7 - n_batch, d), F32)], axis=0)
    mods = _mod_call(cvec, w_mod, b_mod).reshape(depth, 8, N_MOD, d)

    bd = jnp.asarray(np.kron(np.eye(d_rwkv // RWKV_DH), np.ones((RWKV_DH, RWKV_DH))), BF16)

    xa = (ctx.reshape(n_batch * n_ctx, d), x.reshape(n_batch * seq, d))
    tm = _row_tile(n_batch * n_ctx, seq, 1024)
    tm_s = _row_tile(n_batch * n_ctx, seq, 512)
    tiles = dict(tm=tm, ctx_tiles=n_batch * n_ctx // tm, tiles_per_batch=seq // tm, n_batch=n_batch)
    tiles_s = dict(tm=tm_s, ctx_tiles=n_batch * n_ctx // tm_s, tiles_per_batch=seq // tm_s, n_batch=n_batch)

    for l in range(depth):
        last = l == depth - 1
        bf = lambda w: w.astype(BF16)
        xa = _ffn_call(xa, mods[l], norm_gains[l, 0], bf(ffn1_up[l, :, :d_ff]), bf(ffn1_up[l, :, d_ff:]),
                       bf(ffn1_down[l]), mod_base=0, **tiles)
        p_hg, p_rw = _inproj_call(xa, mods[l], norm_gains[l, 1], bf(w_in[l, :, :hg_cols]),
                                  bf(w_in[l, :, hg_cols:]), rwkv_conv[l].reshape(9, -1), nc=nc, nl=nl,
                                  **tiles_s)
        vec = jnp.concatenate([rwkv_w0[l], rwkv_a0[l], rwkv_k_k[l][None], rwkv_k_a[l][None],
                               rwkv_r_k[l].reshape(1, d_rwkv), jnp.zeros((1, d_rwkv), F32)], axis=0)
        o_f, o_b, y_f, y_b, bonus, g_r = _mix_call(
            p_hg, p_rw, lbp[l], vec, bf(rwkv_w_up[l]), bf(rwkv_a_up[l]), bf(rwkv_g_up[l]), bd,
            n_batch=n_batch, nc=nc, nl=nl)
        vec2 = jnp.concatenate([hgrn_o_gain[l][None], rwkv_ln_w[l][None], rwkv_ln_b[l][None],
                                jnp.zeros((5, d_rwkv), F32)], axis=0)
        xa = _post_call(xa, mods[l], o_f, o_b, p_hg, y_f, y_b, bonus, g_r, vec2, bd, bf(w_out[l]),
                        **tiles)
        up, down = bf(ffn2_up[l]), bf(ffn2_down[l])
        xa = _ffn_call(xa, mods[l], norm_gains[l, 2], up[:, :d_ff], up[:, d_ff:], down, mod_base=6,
                       skip_ctx=last, final_gain=final_gain if last else None, **tiles)
    return xa.reshape(n_batch, seq, d)
```

```python
import functools
import math

import numpy as np
import jax
import jax.numpy as jnp
from jax import lax
from jax.experimental import pallas as pl
from jax.experimental.pallas import tpu as pltpu

F32 = jnp.float32
BF16 = jnp.bfloat16

CHUNK = 64
GRID_W = 64
HGRN_HEADS = 4
RWKV_DH = 64
LANES = 128
W_RANK = 64
A_RANK = 64
G_RANK = 128
N_MOD = 9
RMS_EPS = 1e-6
GN_EPS = 64e-5
VMEM_LIMIT = 56 * 1024 * 1024


def _cparams(*sem):
    return pltpu.CompilerParams(dimension_semantics=sem, vmem_limit_bytes=VMEM_LIMIT)


def _dot(a, b):
    return jnp.dot(a.astype(BF16), b.astype(BF16), preferred_element_type=F32)


def _dot_nt(a, b):
    return lax.dot_general(a.astype(BF16), b.astype(BF16), (((1,), (1,)), ((), ())),
                           preferred_element_type=F32)


def _dot_tn(a, b):
    return lax.dot_general(a.astype(BF16), b.astype(BF16), (((0,), (0,)), ((), ())),
                           preferred_element_type=F32)


def _split2(x):
    hi = x.astype(BF16)
    return hi, (x - hi.astype(F32)).astype(BF16)


def _dot01(m01, x):
    hi, lo = _split2(x)
    return (jnp.dot(m01, hi, preferred_element_type=F32)
            + jnp.dot(m01, lo, preferred_element_type=F32))


def _dot_x01(x, m01):
    hi, lo = _split2(x)
    return (jnp.dot(hi, m01, preferred_element_type=F32)
            + jnp.dot(lo, m01, preferred_element_type=F32))


def _sigmoid(x):
    return 0.5 * jnp.tanh(0.5 * x) + 0.5


def _silu(x):
    return x * _sigmoid(x)


def _rms_mod(x, gain, shift, scale):
    y = x * lax.rsqrt(jnp.mean(x * x, axis=-1, keepdims=True) + RMS_EPS)
    return (y * gain) * (1.0 + scale) + shift


def _mod_kernel(c_ref, w_ref, b_ref, o_ref):
    s = _silu(c_ref[...])
    o_ref[0] = jnp.dot(s, w_ref[0], preferred_element_type=F32,
                       precision=lax.Precision.HIGHEST) + b_ref[0]


def _mod_call(cvec, w_mod, b_mod):
    depth, d, n = w_mod.shape
    tn = 1152 if n % 1152 == 0 else n
    return pl.pallas_call(
        _mod_kernel,
        grid=(depth, n // tn),
        in_specs=[pl.BlockSpec((8, d), lambda l, j: (0, 0)),
                  pl.BlockSpec((1, d, tn), lambda l, j: (l, 0, j)),
                  pl.BlockSpec((1, 1, tn), lambda l, j: (l, 0, j))],
        out_specs=pl.BlockSpec((1, 8, tn), lambda l, j: (l, 0, j)),
        out_shape=jax.ShapeDtypeStruct((depth, 8, n), F32),
        compiler_params=_cparams("arbitrary", "arbitrary"),
        name="mod_vectors",
    )(cvec, w_mod, b_mod.reshape(depth, 1, n))


def _row_tile(n_ctx_rows, seq, cap):
    tm = cap
    while n_ctx_rows % tm or seq % tm:
        tm //= 2
    return tm


def _mod_index(i, ctx_tiles, tiles_per_batch, n_batch):
    return jnp.where(i < ctx_tiles, n_batch, (i - ctx_tiles) // tiles_per_batch)


def _resident(shape):
    return pl.BlockSpec(shape, lambda *_: (0,) * len(shape), pipeline_mode=pl.Buffered(1))


def _layer_block(shape, layer, col=0):
    return pl.BlockSpec((None, *shape), lambda *_: (layer, 0, col), pipeline_mode=pl.Buffered(1))


def _ffn_kernel(*refs, mod_base, ff_chunk, final, split_at):
    if split_at:
        c_ref, x_ref, *refs = refs
        x = jnp.where(pl.program_id(0) < split_at, c_ref[...], x_ref[...])
    else:
        x_ref, *refs = refs
        x = x_ref[...]
    m_ref, g_ref, wg_ref, wu_ref, wd_ref, *rest = refs
    if final:
        fg_ref, o_ref = rest
    else:
        (o_ref,) = rest
    m = m_ref[0]
    h = _rms_mod(x, g_ref[...], m[mod_base:mod_base + 1], m[mod_base + 1:mod_base + 2]).astype(BF16)
    d_ff = wd_ref.shape[0]
    acc = jnp.zeros(x.shape, F32)
    for j in range(d_ff // ff_chunk):
        sl = slice(j * ff_chunk, (j + 1) * ff_chunk)
        gate = jnp.dot(h, wg_ref[:, sl], preferred_element_type=F32)
        up = jnp.dot(h, wu_ref[:, sl], preferred_element_type=F32)
        act = (_silu(gate) * up).astype(BF16)
        acc = acc + jnp.dot(act, wd_ref[sl, :], preferred_element_type=F32)
    y = x + (0.5 * m[mod_base + 2:mod_base + 3]) * acc
    if final:
        y = (y * lax.rsqrt(jnp.mean(y * y, axis=-1, keepdims=True) + RMS_EPS)) * fg_ref[...]
    o_ref[...] = y


def _ffn_call(xa, mods, gain, w_up, w_down, layer, *, mod_base, tm, ctx_tiles, tiles_per_batch, n_batch,
              skip_ctx=False, final_gain=None):
    split = isinstance(xa, tuple)
    _, d_ff, d = w_down.shape
    rows = sum(a.shape[0] for a in xa) if split else xa.shape[0]
    n_tiles = rows // tm
    t0 = ctx_tiles if skip_ctx else 0
    final = final_gain is not None
    if split:
        x_specs = [pl.BlockSpec((tm, d), lambda i: (jnp.minimum(i, ctx_tiles - 1), 0)),
                   pl.BlockSpec((tm, d), lambda i: (jnp.maximum(i - ctx_tiles, 0), 0))]
    else:
        x_specs = [pl.BlockSpec((tm, d), lambda i: (i + t0, 0))]
    in_specs = x_specs + [
        pl.BlockSpec((1, N_MOD, d),
                     lambda i: (_mod_index(i + t0, ctx_tiles, tiles_per_batch, n_batch), 0, 0)),
        _resident((1, d)), _layer_block((d, d_ff), layer, 0), _layer_block((d, d_ff), layer, 1),
        _layer_block((d_ff, d), layer)]
    args = [*(xa if split else (xa,)), mods, gain.reshape(1, d), w_up, w_up, w_down]
    if final:
        in_specs.append(_resident((1, d)))
        args.append(final_gain.reshape(1, d))
    out_rows = rows - t0 * tm
    return pl.pallas_call(
        functools.partial(_ffn_kernel, mod_base=mod_base, ff_chunk=256, final=final,
                          split_at=ctx_tiles if split else 0),
        grid=(n_tiles - t0,),
        in_specs=in_specs,
        out_specs=pl.BlockSpec((tm, d), lambda i: (i, 0)),
        out_shape=jax.ShapeDtypeStruct((out_rows, d), F32),
        compiler_params=_cparams("arbitrary"),
        name="ffn_half_step",
    )(*args)


def _inproj_kernel(x_ref, m_ref, g_ref, w_ref, ohg_ref, orw_ref):
    m = m_ref[0]
    h = _rms_mod(x_ref[...], g_ref[...], m[3:4], m[4:5]).astype(BF16)
    n_hg = ohg_ref.shape[1]
    ohg_ref[...] = jnp.dot(h, w_ref[:, :n_hg], preferred_element_type=F32)
    orw_ref[...] = jnp.dot(h, w_ref[:, n_hg:], preferred_element_type=F32)


def _inproj_call(xa, mods, gain, w_in, layer, n_hg, *, tm, ctx_tiles, tiles_per_batch, n_batch):
    rows, d = xa.shape
    n_rw = w_in.shape[2] - n_hg
    return pl.pallas_call(
        _inproj_kernel,
        grid=(rows // tm,),
        in_specs=[pl.BlockSpec((tm, d), lambda i: (i, 0)),
                  pl.BlockSpec((1, N_MOD, d),
                               lambda i: (_mod_index(i, ctx_tiles, tiles_per_batch, n_batch), 0, 0)),
                  _resident((1, d)), _layer_block((d, n_hg + n_rw), layer)],
        out_specs=[pl.BlockSpec((tm, n_hg), lambda i: (i, 0)),
                   pl.BlockSpec((tm, n_rw), lambda i: (i, 0))],
        out_shape=[jax.ShapeDtypeStruct((rows, n_hg), F32), jax.ShapeDtypeStruct((rows, n_rw), F32)],
        compiler_params=_cparams("arbitrary"),
        name="in_projection",
    )(xa, mods, gain.reshape(1, d), w_in)


def _chunk_maps(n_batch, nc, nl):
    nt = nc + nl

    def rb(b, c):
        return jnp.where(c < nc, b * nc + c, n_batch * nc + b * nl + (c - nc))

    def c_fwd(i):
        return i

    def c_bwd(i):
        return jnp.where(i < nc, nc - 1 - i, nt + nc - 1 - i)

    return rb, c_fwd, c_bwd


def _interleave(*stages):
    values = [None] * len(stages)
    live = list(range(len(stages)))
    while live:
        for g in tuple(live):
            try:
                next(stages[g])
            except StopIteration as stop:
                values[g] = stop.value
                live.remove(g)
    return values


def _tri(rev, strict):
    r = lax.broadcasted_iota(jnp.int32, (CHUNK, CHUNK), 0)
    c = lax.broadcasted_iota(jnp.int32, (CHUNK, CHUNK), 1)
    if rev:
        return (r < c) if strict else (r <= c)
    return (r > c) if strict else (r >= c)


def _gla_tables():
    n = CHUNK
    idx = np.arange(n)
    incl = (idx[:, None] >= idx[None, :])
    rows = [incl, (idx[None, :] > idx[:, None])]
    masks = [np.eye(n, dtype=bool)]
    h = n // 2
    while h >= 1:
        blk = idx // (2 * h)
        mid = blk * 2 * h + h
        rowpart = (idx % (2 * h)) >= h
        as_row = (idx[None, :] > mid[:, None]) & (idx[None, :] <= idx[:, None]) & rowpart[:, None]
        as_col = (idx[None, :] > idx[:, None]) & (idx[None, :] <= mid[:, None]) & (~rowpart)[:, None]
        rows.append(as_row | as_col)
        masks.append((blk[:, None] == blk[None, :]) & rowpart[:, None] & (~rowpart)[None, :])
        h //= 2
    fwd = np.concatenate(rows, axis=0).astype(np.float32)
    mask_f = np.stack(masks).astype(np.float32)
    n_blocks = fwd.shape[0] // n
    rev = fwd.reshape(n_blocks, n, n)[:, ::-1, ::-1].reshape(fwd.shape)
    mask_r = mask_f[:, ::-1, ::-1]
    return np.stack([fwd, rev]), np.stack([mask_f, mask_r])


def _gla_gates(inputs, lbp_ref, tab_ref, g_ref, e_ref):
    dk = g_ref.shape[3] // HGRN_HEADS
    for d, (q_ref, z_ref, v_ref) in enumerate(inputs):
        for h in range(HGRN_HEADS):
            sl = slice(h * dk, (h + 1) * dk)
            lbp = lbp_ref[d, :, sl]
            log_lb, log_1mlb, one_mlb = lbp[0:1], lbp[1:2], lbp[2:3]
            z = z_ref[:, sl]
            log_sig = jnp.minimum(z, 0.0) - jnp.log(1.0 + jnp.exp(-jnp.abs(z)))
            bv = log_1mlb + log_sig
            lf = jnp.maximum(log_lb, bv) + jnp.log(1.0 + jnp.exp(-jnp.abs(log_lb - bv)))
            g_ref[d, 0, :, sl] = _silu(q_ref[:, sl])
            g_ref[d, 1, :, sl] = one_mlb * _sigmoid(-z)
            g_ref[d, 2, :, sl] = v_ref[:, sl]
            e_ref[d, :, sl] = jnp.exp(_dot01(tab_ref[d], lf))
            yield


def _gla_scan(g_ref, e_ref, mask_ref, s_ref):
    c = CHUNK
    n_levels = mask_ref.shape[1] - 1
    bf = lambda t: t.astype(BF16)
    mm = lambda a, b: jnp.dot(a, b, preferred_element_type=F32)
    mm_nt = lambda a, b: lax.dot_general(a, b, (((1,), (1,)), ((), ())), preferred_element_type=F32)
    mm_tn = lambda a, b: lax.dot_general(a, b, (((0,), (0,)), ((), ())), preferred_element_type=F32)
    dk = g_ref.shape[3] // HGRN_HEADS
    chains = []
    for d in range(2):
        masks = mask_ref[d]
        for h in range(HGRN_HEADS):
            sl = slice(h * dk, (h + 1) * dk)
            qh, kh, ex = g_ref[d, 0, :, sl], g_ref[d, 1, :, sl], e_ref[d, :, sl]
            ch = dict(d=d, h=h, v=bf(g_ref[d, 2, :, sl]),
                      q_in=bf(qh * ex[0:c]), k_out=bf(kh * ex[c:2 * c]),
                      w_tot=ex[0:1] if d == 1 else ex[c - 1:c])
            att = masks[0] * mm_nt(bf(qh), bf(kh))
            for l in range(n_levels):
                lv = ex[(2 + l) * c:(3 + l) * c]
                att = att + masks[l + 1] * mm_nt(bf(qh * lv), bf(kh * lv))
            ch["att"] = bf(att)
            chains.append(ch)
            yield
    outs = [[None] * HGRN_HEADS for _ in range(2)]
    for ch in chains:
        st = s_ref[ch["d"], ch["h"]]
        outs[ch["d"]][ch["h"]] = mm(ch["att"], ch["v"]) + mm_nt(ch["q_in"], bf(st))
        s_ref[ch["d"], ch["h"]] = st * ch["w_tot"] + mm_tn(ch["v"], ch["k_out"])
    return [jnp.concatenate(o, axis=1) for o in outs]


def _shift_conv(prev_ref, cur_ref, nxt_ref, conv_ref, out_ref, is_lat, has_prev, has_next):
    n, width = cur_ref.shape
    row = lax.broadcasted_iota(jnp.int32, (n, LANES), 0)
    lat = jnp.where(is_lat, 1.0, 0.0)
    hp = jnp.where(has_prev, 1.0, 0.0)
    hn = jnp.where(has_next, 1.0, 0.0)
    edge_l = (1.0 - lat) * hp
    edge_r = (1.0 - lat) * hn
    for s in range(width // LANES):
        sl = slice(s * LANES, (s + 1) * LANES)
        kern = conv_ref[:, sl]
        prev, cur, nxt = prev_ref[:, sl], cur_ref[:, sl], nxt_ref[:, sl]
        k_up = kern[0:3] * (lat * hp)
        k_dn = kern[6:9] * (lat * hn)

        def column(j):
            return prev * k_up[j:j + 1] + cur * kern[3 + j:4 + j] + nxt * k_dn[j:j + 1]

        fill_l = prev[n - 1:n] * (edge_l * kern[3:4])
        fill_r = nxt[0:1] * (edge_r * kern[5:6])
        out_ref[:, sl] = (column(1)
                          + jnp.where(row == 0, fill_l, pltpu.roll(column(0), 1, 0))
                          + jnp.where(row == n - 1, fill_r, pltpu.roll(column(2), n - 1, 0)))
        if s == width // LANES - 1:
            yield


def _shift_kernel(prev_ref, cur_ref, nxt_ref, conv_ref, o_ref, *, ctx_chunks, nc, nl):
    tm = cur_ref.shape[0]
    cpt = tm // CHUNK
    first = pl.program_id(0) * cpt
    for c in range(cpt):
        g = first + c
        is_lat = g >= ctx_chunks
        pos = jnp.where(is_lat, (g - ctx_chunks) % nl, g % nc)
        has_prev = pos != 0
        has_next = pos != jnp.where(is_lat, nl - 1, nc - 1)
        prev = cur_ref.at[(c - 1) * CHUNK:c * CHUNK] if c else prev_ref
        nxt = cur_ref.at[(c + 1) * CHUNK:(c + 2) * CHUNK] if c < cpt - 1 else nxt_ref
        for _ in _shift_conv(prev, cur_ref.at[c * CHUNK:(c + 1) * CHUNK], nxt, conv_ref,
                             o_ref.at[c * CHUNK:(c + 1) * CHUNK], is_lat, has_prev, has_next):
            pass


def _shift_call(p_rw, conv, *, tm, n_batch, nc, nl):
    rows, n_rw = p_rw.shape
    cpt = tm // CHUNK
    last = rows // CHUNK - 1
    return pl.pallas_call(
        functools.partial(_shift_kernel, ctx_chunks=n_batch * nc, nc=nc, nl=nl),
        grid=(rows // tm,),
        in_specs=[pl.BlockSpec((CHUNK, n_rw), lambda i: (jnp.maximum(i * cpt - 1, 0), 0)),
                  pl.BlockSpec((tm, n_rw), lambda i: (i, 0)),
                  pl.BlockSpec((CHUNK, n_rw), lambda i: (jnp.minimum((i + 1) * cpt, last), 0)),
                  _resident(conv.shape)],
        out_specs=pl.BlockSpec((tm, n_rw), lambda i: (i, 0)),
        out_shape=jax.ShapeDtypeStruct((rows, n_rw), F32),
        compiler_params=_cparams("arbitrary"),
        name="token_shift",
    )(p_rw, p_rw, p_rw, conv)


def _pair_diag(x):
    lo = lax.broadcasted_iota(jnp.int32, x.shape, 1) < RWKV_DH
    z = jnp.zeros_like(x)
    return jnp.concatenate([jnp.where(lo, x, z), jnp.where(lo, z, x)], axis=0)


def _rwkv_scan_chunks(q_ref, s_ref):
    c = CHUNK
    pw = 2 * RWKV_DH
    n_pairs = q_ref.shape[3] // pw
    row = lax.broadcasted_iota(jnp.int32, (c, pw), 0)
    lane = lax.broadcasted_iota(jnp.int32, (c, pw), 1)
    col = lane & (RWKV_DH - 1)
    lo = lane < RWKV_DH
    eye = row == col
    bf = lambda t: t.astype(BF16)
    mm = lambda a, b: jnp.dot(a, b, preferred_element_type=F32)
    mm_nt = lambda a, b: lax.dot_general(a, b, (((1,), (1,)), ((), ())), preferred_element_type=F32)
    mm_tn = lambda a, b: lax.dot_general(a, b, (((0,), (0,)), ((), ())), preferred_element_type=F32)

    chains = []
    for d in range(2):
        rev = d == 1
        tri = (row <= col) if rev else (row >= col)
        strict = (row < col) if rev else (row > col)
        tri01 = jnp.where(_tri(rev, False), 1.0, 0.0).astype(BF16)
        for j in range(n_pairs):
            sl = slice(j * pw, (j + 1) * pw)
            r, lw, k, v, kk, a = (q_ref[d, t, :, sl] for t in range(6))
            b_incl = _dot01(tri01, lw)
            b_tot = b_incl[0:1] if rev else b_incl[c - 1:c]
            e_in = jnp.exp(-b_incl)
            e_out = jnp.exp(b_tot - b_incl)
            beta = kk * a
            chains.append(dict(
                d=d, j=j, tri=tri, strict=strict, w_tot=jnp.exp(b_tot), v=v,
                ar=bf(jnp.concatenate([-kk * jnp.exp(b_incl - lw), r * jnp.exp(b_incl)], axis=0)),
                bt=_pair_diag(bf(beta * e_in)), kt=_pair_diag(bf(k * e_in)),
                bk=bf(jnp.concatenate([beta * e_out, k * e_out], axis=0))))
    yield
    for ch in chains:
        g_b = mm_nt(ch["ar"], ch["bt"])
        g_k = mm_nt(ch["ar"], ch["kt"])
        a_ab = jnp.where(ch["strict"], g_b[0:c], 0.0)
        ch["a_rb"] = bf(jnp.where(ch["tri"], g_b[c:], 0.0))
        ch["a_k"] = bf(jnp.concatenate([jnp.where(ch["strict"], g_k[0:c], 0.0),
                                        jnp.where(ch["tri"], g_k[c:], 0.0)], axis=0))
        ch["p"] = jnp.where(eye, 1.0, a_ab)
        ch["ak"] = bf(a_ab)
    yield
    for ch in chains:
        ch["ak2"] = mm(ch["ak"], _pair_diag(ch["ak"]))
    yield
    n_lv = int(math.log2(c))
    for lv in range(2, n_lv + 1):
        for ch in chains:
            ak = bf(ch["ak2"])
            p_bd = _pair_diag(bf(ch["p"]))
            if lv < n_lv:
                both = mm(ak, jnp.concatenate([_pair_diag(ak), p_bd], axis=1))
                ch["ak2"] = both[:, :pw]
                ch["p"] = ch["p"] + both[:, pw:]
            else:
                ch["p"] = ch["p"] + mm(ak, p_bd)
        yield
    for ch in chains:
        s0 = s_ref[ch["d"], ch["j"]]
        ch["s0"] = s0
        ch["xs"] = mm_nt(ch["ar"], _pair_diag(bf(s0)))
        ch["av"] = mm(ch["a_k"], _pair_diag(bf(ch["v"])))
    yield
    for ch in chains:
        x = ch["xs"][0:c] + ch["av"][0:c]
        ch["u"] = mm(bf(ch["p"]), _pair_diag(bf(x)))
    yield
    outs = [[None] * n_pairs for _ in range(2)]
    for ch in chains:
        u = ch["u"]
        outs[ch["d"]][ch["j"]] = ch["xs"][c:] + ch["av"][c:] + mm(ch["a_rb"], _pair_diag(bf(u)))
        z = mm_tn(bf(jnp.concatenate([u, ch["v"]], axis=0)), ch["bk"])
        s_ref[ch["d"], ch["j"]] = ch["s0"] * ch["w_tot"] + jnp.where(lo, z[0:c], z[c:])
    return [jnp.concatenate(o, axis=1) for o in outs]


def _rwkv_prep(rw_ref, vec_ref, wup_ref, aup_ref, gup_ref, bd_ref, q_ref, bonus_ref, gr_ref, d):
    dr = q_ref.shape[3]
    o_w = 3 * dr
    o_a = o_w + 2 * W_RANK
    o_g = o_a + 2 * A_RANK
    bd = bd_ref[0:LANES, 0:LANES]
    tanh_wd = jnp.tanh(rw_ref[:, o_w + d * W_RANK:o_w + (d + 1) * W_RANK]).astype(BF16)
    ad = [rw_ref[:, o_a + e * A_RANK:o_a + (e + 1) * A_RANK].astype(BF16) for e in range(2)]
    if d == 0:
        sig_gd = _sigmoid(rw_ref[:, o_g:o_g + G_RANK]).astype(BF16)
    mm = lambda a, b: jnp.dot(a, b, preferred_element_type=F32)
    for j in range(dr // LANES):
        sl = slice(j * LANES, (j + 1) * LANES)
        vec = vec_ref[:, sl]
        w0, a0 = vec[0:2], vec[2:4]
        k_k, k_a, r_k = vec[4:5], vec[5:6], vec[6:7]
        r = rw_ref[:, j * LANES:(j + 1) * LANES]
        k = rw_ref[:, dr + j * LANES:dr + (j + 1) * LANES]
        v = rw_ref[:, 2 * dr + j * LANES:2 * dr + (j + 1) * LANES]
        lw = -math.exp(-0.5) * _sigmoid(w0[d:d + 1] + mm(tanh_wd, wup_ref[d, :, sl]))
        a = _sigmoid(a0[d:d + 1] + mm(ad[d], aup_ref[d, :, sl]))
        kk = k * k_k
        kk = kk / jnp.maximum(jnp.sqrt(_dot_x01(kk * kk, bd)), 1e-12)
        for t, val in enumerate((r, lw, k * (1.0 + (a - 1.0) * k_a), v, kk, a)):
            q_ref[d, t, :, sl] = val
        if d == 0:
            a_b = _sigmoid(a0[1:2] + mm(ad[1], aup_ref[1, :, sl]))
            k_sum = k * (2.0 + (a + a_b - 2.0) * k_a)
            bonus_ref[:, sl] = (_dot_x01(r * k_sum * r_k, bd) * v).astype(bonus_ref.dtype)
            gr_ref[:, sl] = mm(sig_gd, gup_ref[:, sl]).astype(gr_ref.dtype)
        if j % 2 == 1:
            yield


def _mix_kernel(qf_ref, zf_ref, vf_ref, qb_ref, zb_ref, vb_ref, lbp_ref, tab_ref, mask_ref,
                rwf_ref, rwb_ref, vec_ref, wup_ref, aup_ref, gup_ref,
                bd_ref, of_ref, ob_ref, yf_ref, yb_ref, bonus_ref, gr_ref,
                hs_ref, g_ref, e_ref, s_ref, q_ref):
    step = pl.program_id(1)

    @pl.when(step == 0)
    def _():
        g_ref[...] = jnp.zeros(g_ref.shape, F32)
        e_ref[...] = jnp.zeros(e_ref.shape, F32)
        q_ref[...] = jnp.zeros(q_ref.shape, F32)

    @pl.when(step <= 1)
    def _():
        hs_ref[...] = jnp.zeros(hs_ref.shape, F32)
        s_ref[...] = jnp.zeros(s_ref.shape, F32)

    scan = _rwkv_scan_chunks(q_ref, s_ref)
    next(scan)

    def prepare():
        for d, rw_ref in enumerate((rwf_ref, rwb_ref)):
            yield from _rwkv_prep(rw_ref, vec_ref, wup_ref, aup_ref, gup_ref, bd_ref, q_ref, bonus_ref,
                                  gr_ref, d)

    (y_f, y_b), _, (o_f, o_b), _ = _interleave(
        scan, prepare(), _gla_scan(g_ref, e_ref, mask_ref, hs_ref),
        _gla_gates(((qf_ref, zf_ref, vf_ref), (qb_ref, zb_ref, vb_ref)), lbp_ref, tab_ref, g_ref, e_ref))
    for ref, val in ((yf_ref, y_f), (yb_ref, y_b), (of_ref, o_f), (ob_ref, o_b)):
        ref[...] = val.astype(ref.dtype)


def _mix_call(p_hg, p_rw, lbp, vec, w_up, a_up, g_up, bd, *, n_batch, nc, nl):
    rows, n_rw = p_rw.shape
    dh = p_hg.shape[1] // 5
    dr = bd.shape[0]
    nt = nc + nl
    rb, c_fwd, c_bwd = _chunk_maps(n_batch, nc, nl)
    tab, masks = _gla_tables()
    tab = jnp.asarray(tab, BF16)
    masks = jnp.asarray(masks, F32)
    nxt = lambda i: jnp.minimum(i, nt - 1)
    cur = lambda i: jnp.maximum(i - 1, 0)

    def hg(col, cmap):
        return pl.BlockSpec((CHUNK, dh), lambda b, i: (rb(b, cmap(nxt(i))), col))

    def rw(cmap):
        return pl.BlockSpec((CHUNK, n_rw), lambda b, i: (rb(b, cmap(nxt(i))), 0))

    def out(width, cmap, when):
        return pl.BlockSpec((CHUNK, width), lambda b, i: (rb(b, cmap(when(i))), 0))

    dk = dh // HGRN_HEADS
    n_heads = dr // RWKV_DH
    consts = (lbp, tab, masks)
    consts2 = (vec, w_up, a_up, g_up, bd)
    return pl.pallas_call(
        _mix_kernel,
        grid=(n_batch, nt + 1),
        in_specs=[hg(0, c_fwd), hg(1, c_fwd), hg(3, c_fwd), hg(0, c_bwd), hg(2, c_bwd), hg(3, c_bwd)]
        + [_resident(a.shape) for a in consts]
        + [rw(c_fwd), rw(c_bwd)]
        + [_resident(a.shape) for a in consts2],
        out_specs=[out(dh, c_fwd, cur), out(dh, c_bwd, cur), out(dr, c_fwd, cur), out(dr, c_bwd, cur),
                   out(dr, c_fwd, nxt), out(dr, c_fwd, nxt)],
        out_shape=[jax.ShapeDtypeStruct((rows, dh), BF16)] * 2 + [jax.ShapeDtypeStruct((rows, dr), BF16)] * 4,
        scratch_shapes=[pltpu.VMEM((2, HGRN_HEADS, dk, dk), F32),
                        pltpu.VMEM((2, 3, CHUNK, dh), F32),
                        pltpu.VMEM((2, tab.shape[1], dh), F32),
                        pltpu.VMEM((2, n_heads // 2, RWKV_DH, 2 * RWKV_DH), F32),
                        pltpu.VMEM((2, 6, CHUNK, dr), F32)],
        compiler_params=_cparams("arbitrary", "arbitrary"),
        name="token_mix_scans",
    )(p_hg, p_hg, p_hg, p_hg, p_hg, p_hg, *consts, p_rw, p_rw, *consts2)


def _post_kernel(x_ref, m_ref, of_ref, ob_ref, go_ref, yf_ref, yb_ref, bonus_ref, gr_ref, vec_ref,
                 bd_ref, wo_ref, o_ref):
    vec = vec_ref[...]
    o_gain, ln_w, ln_b = vec[0:1], vec[1:2], vec[2:3]
    f32 = lambda ref: ref[...].astype(F32)
    o = f32(of_ref) + f32(ob_ref)
    dh = o.shape[1] // HGRN_HEADS
    parts = []
    for h in range(HGRN_HEADS):
        oh = o[:, h * dh:(h + 1) * dh]
        parts.append(oh * lax.rsqrt(jnp.mean(oh * oh, axis=-1, keepdims=True) + RMS_EPS))
    o = jnp.concatenate(parts, axis=1) * o_gain * _silu(go_ref[...])
    y = f32(yf_ref) + f32(yb_ref)
    bd = bd_ref[...]
    inv = 1.0 / RWKV_DH
    mu = _dot_x01(y, bd) * inv
    yc = y - mu
    var = _dot_x01(yc * yc, bd) * inv
    y = yc * lax.rsqrt(var + GN_EPS) * ln_w + ln_b
    y = (y + f32(bonus_ref)) * f32(gr_ref)
    mixed = jnp.concatenate([o, y], axis=1).astype(BF16)
    out = jnp.dot(mixed, wo_ref[...], preferred_element_type=F32)
    o_ref[...] = x_ref[...] + m_ref[0][5:6] * out


def _post_call(xa, mods, o_f, o_b, p_hg, y_f, y_b, bonus, g_r, vec, bd, w_out, layer, *, tm, ctx_tiles,
               tiles_per_batch, n_batch):
    rows, d = xa.shape
    dh = o_f.shape[1]
    dr = y_f.shape[1]
    row = lambda w: pl.BlockSpec((tm, w), lambda i: (i, 0))
    return pl.pallas_call(
        _post_kernel,
        grid=(rows // tm,),
        in_specs=[row(d),
                  pl.BlockSpec((1, N_MOD, d),
                               lambda i: (_mod_index(i, ctx_tiles, tiles_per_batch, n_batch), 0, 0)),
                  row(dh), row(dh), pl.BlockSpec((tm, dh), lambda i: (i, 4)),
                  row(dr), row(dr), row(dr), row(dr),
                  _resident(vec.shape), _resident(bd.shape), _layer_block(w_out.shape[1:], layer)],
        out_specs=row(d),
        out_shape=jax.ShapeDtypeStruct((rows, d), F32),
        compiler_params=_cparams("arbitrary"),
        name="mix_out_projection",
    )(xa, mods, o_f, o_b, p_hg, y_f, y_b, bonus, g_r, vec, bd, w_out)


def kernel(x, c, ctx, c_ctx, w_mod, b_mod, norm_gains, final_gain, ffn1_up, ffn1_down, ffn2_up, ffn2_down, w_in, w_out, hgrn_lb_logits, hgrn_o_gain, rwkv_conv, rwkv_w0, rwkv_w_up, rwkv_a0, rwkv_a_up, rwkv_g_up, rwkv_k_k, rwkv_k_a, rwkv_r_k, rwkv_ln_w, rwkv_ln_b):
    n_batch, seq, d = x.shape
    n_ctx = ctx.shape[1]
    depth = w_mod.shape[0]
    d_hgrn = hgrn_o_gain.shape[1]
    d_rwkv = rwkv_k_k.shape[1]
    hg_cols = 5 * d_hgrn
    assert seq % CHUNK == 0 and n_ctx % CHUNK == 0 and GRID_W == CHUNK and n_batch < 8
    nc, nl = n_ctx // CHUNK, seq // CHUNK

    p = jax.nn.softmax(hgrn_lb_logits.astype(F32), axis=0)
    cum = jnp.cumsum(p, axis=0)
    lb = cum - cum[0]
    lbp = jnp.stack([jnp.log(lb), jnp.log1p(-lb), 1.0 - lb], axis=2)

    cvec = jnp.concatenate([c, c_ctx[None], jnp.zeros((7 - n_batch, d), F32)], axis=0)
    mods = _mod_call(cvec, w_mod, b_mod).reshape(depth, 8, N_MOD, d)

    bd = jnp.asarray(np.kron(np.eye(d_rwkv // RWKV_DH), np.ones((RWKV_DH, RWKV_DH))), BF16)

    xa = (ctx.reshape(n_batch * n_ctx, d), x.reshape(n_batch * seq, d))
    tm = _row_tile(n_batch * n_ctx, seq, 1024)
    tm_s = _row_tile(n_batch * n_ctx, seq, 512)
    tiles = dict(tm=tm, ctx_tiles=n_batch * n_ctx // tm, tiles_per_batch=seq // tm, n_batch=n_batch)
    tiles_s = dict(tm=tm_s, ctx_tiles=n_batch * n_ctx // tm_s, tiles_per_batch=seq // tm_s, n_batch=n_batch)

    bf = lambda w: w.astype(BF16)
    up1, down1, up2, down2 = bf(ffn1_up), bf(ffn1_down), bf(ffn2_up), bf(ffn2_down)
    w_in_b, w_out_b = bf(w_in), bf(w_out)
    for l in range(depth):
        last = l == depth - 1
        xa = _ffn_call(xa, mods[l], norm_gains[l, 0], up1, down1, l, mod_base=0, **tiles)
        p_hg, p_rw = _inproj_call(xa, mods[l], norm_gains[l, 1], w_in_b, l, hg_cols, **tiles_s)
        p_rw = _shift_call(p_rw, rwkv_conv[l].reshape(9, -1), tm=tm_s, n_batch=n_batch, nc=nc, nl=nl)
        vec = jnp.concatenate([rwkv_w0[l], rwkv_a0[l], rwkv_k_k[l][None], rwkv_k_a[l][None],
                               rwkv_r_k[l].reshape(1, d_rwkv), jnp.zeros((1, d_rwkv), F32)], axis=0)
        o_f, o_b, y_f, y_b, bonus, g_r = _mix_call(
            p_hg, p_rw, lbp[l], vec, bf(rwkv_w_up[l]), bf(rwkv_a_up[l]), bf(rwkv_g_up[l]), bd,
            n_batch=n_batch, nc=nc, nl=nl)
        vec2 = jnp.concatenate([hgrn_o_gain[l][None], rwkv_ln_w[l][None], rwkv_ln_b[l][None],
                                jnp.zeros((5, d_rwkv), F32)], axis=0)
        xa = _post_call(xa, mods[l], o_f, o_b, p_hg, y_f, y_b, bonus, g_r, vec2, bd, w_out_b, l, **tiles)
        xa = _ffn_call(xa, mods[l], norm_gains[l, 2], up2, down2, l, mod_base=6,
                       skip_ctx=last, final_gain=final_gain if last else None, **tiles)
    return xa.reshape(n_batch, seq, d)
```

```python
import functools
import math

import numpy as np
import jax
import jax.numpy as jnp
from jax import lax
from jax.experimental import pallas as pl
from jax.experimental.pallas import tpu as pltpu

F32 = jnp.float32
BF16 = jnp.bfloat16

CHUNK = 64
GRID_W = 64
HGRN_HEADS = 4
RWKV_DH = 64
LANES = 128
W_RANK = 64
A_RANK = 64
G_RANK = 128
N_MOD = 9
RMS_EPS = 1e-6
GN_EPS = 64e-5
VMEM_LIMIT = 56 * 1024 * 1024


def _cparams(*sem):
    return pltpu.CompilerParams(dimension_semantics=sem, vmem_limit_bytes=VMEM_LIMIT)


def _dot(a, b):
    return jnp.dot(a.astype(BF16), b.astype(BF16), preferred_element_type=F32)


def _dot_nt(a, b):
    return lax.dot_general(a.astype(BF16), b.astype(BF16), (((1,), (1,)), ((), ())),
                           preferred_element_type=F32)


def _dot_tn(a, b):
    return lax.dot_general(a.astype(BF16), b.astype(BF16), (((0,), (0,)), ((), ())),
                           preferred_element_type=F32)


def _split2(x):
    hi = x.astype(BF16)
    return hi, (x - hi.astype(F32)).astype(BF16)


def _dot01(m01_twice, x):
    return jnp.dot(m01_twice, jnp.concatenate(_split2(x), axis=0), preferred_element_type=F32)


def _dot_x01(x, m01):
    return jnp.dot(jnp.concatenate(_split2(x), axis=1), jnp.concatenate([m01, m01], axis=0),
                   preferred_element_type=F32)


def _sigmoid(x):
    return 0.5 * jnp.tanh(0.5 * x) + 0.5


def _silu(x):
    return x * _sigmoid(x)


def _rms_mod(x, gain, shift, scale):
    y = x * lax.rsqrt(jnp.mean(x * x, axis=-1, keepdims=True) + RMS_EPS)
    return (y * gain) * (1.0 + scale) + shift


def _mod_kernel(c_ref, w_ref, b_ref, o_ref):
    s_hi, s_lo = _split2(_silu(c_ref[...]))
    w_hi, w_lo = _split2(w_ref[0])
    dot = lambda a, b: jnp.dot(a, b, preferred_element_type=F32)
    o_ref[0] = dot(s_hi, w_hi) + dot(s_hi, w_lo) + dot(s_lo, w_hi) + b_ref[0]


def _mod_call(cvec, w_mod, b_mod):
    depth, d, n = w_mod.shape
    tn = 1152 if n % 1152 == 0 else n
    return pl.pallas_call(
        _mod_kernel,
        grid=(depth, n // tn),
        in_specs=[pl.BlockSpec((8, d), lambda l, j: (0, 0)),
                  pl.BlockSpec((1, d, tn), lambda l, j: (l, 0, j)),
                  pl.BlockSpec((1, 1, tn), lambda l, j: (l, 0, j))],
        out_specs=pl.BlockSpec((1, 8, tn), lambda l, j: (l, 0, j)),
        out_shape=jax.ShapeDtypeStruct((depth, 8, n), F32),
        compiler_params=_cparams("arbitrary", "arbitrary"),
        name="mod_vectors",
    )(cvec, w_mod, b_mod.reshape(depth, 1, n))


def _row_tile(n_ctx_rows, seq, cap):
    tm = cap
    while n_ctx_rows % tm or seq % tm:
        tm //= 2
    return tm


def _mod_index(i, ctx_tiles, tiles_per_batch, n_batch):
    return jnp.where(i < ctx_tiles, n_batch, (i - ctx_tiles) // tiles_per_batch)


def _resident(shape):
    return pl.BlockSpec(shape, lambda *_: (0,) * len(shape), pipeline_mode=pl.Buffered(1))


def _layer_block(shape, layer, col=0):
    return pl.BlockSpec((None, *shape), lambda *_: (layer, 0, col), pipeline_mode=pl.Buffered(1))


def _ffn_kernel(*refs, mod_base, ff_chunk, final, split_at):
    if split_at:
        c_ref, x_ref, *refs = refs
        x = jnp.where(pl.program_id(0) < split_at, c_ref[...], x_ref[...])
    else:
        x_ref, *refs = refs
        x = x_ref[...]
    m_ref, g_ref, wg_ref, wu_ref, wd_ref, *rest = refs
    if final:
        fg_ref, o_ref = rest
    else:
        (o_ref,) = rest
    m = m_ref[0]
    h = _rms_mod(x, g_ref[...], m[mod_base:mod_base + 1], m[mod_base + 1:mod_base + 2]).astype(BF16)
    d_ff = wd_ref.shape[0]
    acc = jnp.zeros(x.shape, F32)
    for j in range(d_ff // ff_chunk):
        sl = slice(j * ff_chunk, (j + 1) * ff_chunk)
        gate = jnp.dot(h, wg_ref[:, sl], preferred_element_type=F32)
        up = jnp.dot(h, wu_ref[:, sl], preferred_element_type=F32)
        act = (_silu(gate) * up).astype(BF16)
        acc = acc + jnp.dot(act, wd_ref[sl, :], preferred_element_type=F32)
    y = x + (0.5 * m[mod_base + 2:mod_base + 3]) * acc
    if final:
        y = (y * lax.rsqrt(jnp.mean(y * y, axis=-1, keepdims=True) + RMS_EPS)) * fg_ref[...]
    o_ref[...] = y


def _ffn_call(xa, mods, gain, w_up, w_down, layer, *, mod_base, tm, ctx_tiles, tiles_per_batch, n_batch,
              skip_ctx=False, final_gain=None):
    split = isinstance(xa, tuple)
    _, d_ff, d = w_down.shape
    rows = sum(a.shape[0] for a in xa) if split else xa.shape[0]
    n_tiles = rows // tm
    t0 = ctx_tiles if skip_ctx else 0
    final = final_gain is not None
    if split:
        x_specs = [pl.BlockSpec((tm, d), lambda i: (jnp.minimum(i, ctx_tiles - 1), 0)),
                   pl.BlockSpec((tm, d), lambda i: (jnp.maximum(i - ctx_tiles, 0), 0))]
    else:
        x_specs = [pl.BlockSpec((tm, d), lambda i: (i + t0, 0))]
    in_specs = x_specs + [
        pl.BlockSpec((1, N_MOD, d),
                     lambda i: (_mod_index(i + t0, ctx_tiles, tiles_per_batch, n_batch), 0, 0)),
        _resident((1, d)), _layer_block((d, d_ff), layer, 0), _layer_block((d, d_ff), layer, 1),
        _layer_block((d_ff, d), layer)]
    args = [*(xa if split else (xa,)), mods, gain.reshape(1, d), w_up, w_up, w_down]
    if final:
        in_specs.append(_resident((1, d)))
        args.append(final_gain.reshape(1, d))
    out_rows = rows - t0 * tm
    return pl.pallas_call(
        functools.partial(_ffn_kernel, mod_base=mod_base, ff_chunk=256, final=final,
                          split_at=ctx_tiles if split else 0),
        grid=(n_tiles - t0,),
        in_specs=in_specs,
        out_specs=pl.BlockSpec((tm, d), lambda i: (i, 0)),
        out_shape=jax.ShapeDtypeStruct((out_rows, d), F32),
        compiler_params=_cparams("arbitrary"),
        name="ffn_half_step",
    )(*args)


def _inproj_kernel(x_ref, m_ref, g_ref, w_ref, ohg_ref, orw_ref):
    m = m_ref[0]
    h = _rms_mod(x_ref[...], g_ref[...], m[3:4], m[4:5]).astype(BF16)
    n_hg = ohg_ref.shape[1]
    ohg_ref[...] = jnp.dot(h, w_ref[:, :n_hg], preferred_element_type=F32)
    orw_ref[...] = jnp.dot(h, w_ref[:, n_hg:], preferred_element_type=F32)


def _inproj_call(xa, mods, gain, w_in, layer, n_hg, *, tm, ctx_tiles, tiles_per_batch, n_batch):
    rows, d = xa.shape
    n_rw = w_in.shape[2] - n_hg
    return pl.pallas_call(
        _inproj_kernel,
        grid=(rows // tm,),
        in_specs=[pl.BlockSpec((tm, d), lambda i: (i, 0)),
                  pl.BlockSpec((1, N_MOD, d),
                               lambda i: (_mod_index(i, ctx_tiles, tiles_per_batch, n_batch), 0, 0)),
                  _resident((1, d)), _layer_block((d, n_hg + n_rw), layer)],
        out_specs=[pl.BlockSpec((tm, n_hg), lambda i: (i, 0)),
                   pl.BlockSpec((tm, n_rw), lambda i: (i, 0))],
        out_shape=[jax.ShapeDtypeStruct((rows, n_hg), F32), jax.ShapeDtypeStruct((rows, n_rw), F32)],
        compiler_params=_cparams("arbitrary"),
        name="in_projection",
    )(xa, mods, gain.reshape(1, d), w_in)


def _chunk_maps(n_batch, nc, nl):
    nt = nc + nl

    def rb(b, c):
        return jnp.where(c < nc, b * nc + c, n_batch * nc + b * nl + (c - nc))

    def c_fwd(i):
        return i

    def c_bwd(i):
        return jnp.where(i < nc, nc - 1 - i, nt + nc - 1 - i)

    return rb, c_fwd, c_bwd


def _interleave(*stages):
    values = [None] * len(stages)
    live = list(range(len(stages)))
    while live:
        for g in tuple(live):
            try:
                next(stages[g])
            except StopIteration as stop:
                values[g] = stop.value
                live.remove(g)
    return values


def _gla_tables():
    n = CHUNK
    idx = np.arange(n)
    incl = (idx[:, None] >= idx[None, :])
    rows = [incl, (idx[None, :] > idx[:, None])]
    masks = [np.eye(n, dtype=bool)]
    h = n // 2
    while h >= 1:
        blk = idx // (2 * h)
        mid = blk * 2 * h + h
        rowpart = (idx % (2 * h)) >= h
        as_row = (idx[None, :] > mid[:, None]) & (idx[None, :] <= idx[:, None]) & rowpart[:, None]
        as_col = (idx[None, :] > idx[:, None]) & (idx[None, :] <= mid[:, None]) & (~rowpart)[:, None]
        rows.append(as_row | as_col)
        masks.append((blk[:, None] == blk[None, :]) & rowpart[:, None] & (~rowpart)[None, :])
        h //= 2
    fwd = np.concatenate(rows, axis=0).astype(np.float32)
    mask_f = np.stack(masks).astype(np.float32)
    n_blocks = fwd.shape[0] // n
    rev = fwd.reshape(n_blocks, n, n)[:, ::-1, ::-1].reshape(fwd.shape)
    mask_r = mask_f[:, ::-1, ::-1]
    tables = np.stack([fwd, rev])
    return np.concatenate([tables, tables], axis=-1), np.stack([mask_f, mask_r])


def _gla_gates(inputs, lbp_ref, tab_ref, g_ref, e_ref):
    dk = g_ref.shape[3] // HGRN_HEADS
    for d, (q_ref, z_ref, v_ref) in enumerate(inputs):
        for h in range(HGRN_HEADS):
            sl = slice(h * dk, (h + 1) * dk)
            lbp = lbp_ref[d, :, sl]
            log_lb, log_1mlb, one_mlb = lbp[0:1], lbp[1:2], lbp[2:3]
            z = z_ref[:, sl]
            log_sig = jnp.minimum(z, 0.0) - jnp.log(1.0 + jnp.exp(-jnp.abs(z)))
            bv = log_1mlb + log_sig
            lf = jnp.maximum(log_lb, bv) + jnp.log(1.0 + jnp.exp(-jnp.abs(log_lb - bv)))
            g_ref[d, 0, :, sl] = _silu(q_ref[:, sl])
            g_ref[d, 1, :, sl] = one_mlb * _sigmoid(-z)
            g_ref[d, 2, :, sl] = v_ref[:, sl]
            e_ref[d, :, sl] = jnp.exp(_dot01(tab_ref[d], lf))
            yield


def _gla_scan(g_ref, e_ref, mask_ref, s_ref):
    c = CHUNK
    n_levels = mask_ref.shape[1] - 1
    bf = lambda t: t.astype(BF16)
    mm = lambda a, b: jnp.dot(a, b, preferred_element_type=F32)
    mm_nt = lambda a, b: lax.dot_general(a, b, (((1,), (1,)), ((), ())), preferred_element_type=F32)
    mm_tn = lambda a, b: lax.dot_general(a, b, (((0,), (0,)), ((), ())), preferred_element_type=F32)
    dk = g_ref.shape[3] // HGRN_HEADS
    chains = []
    for d in range(2):
        masks = mask_ref[d]
        for h in range(HGRN_HEADS):
            sl = slice(h * dk, (h + 1) * dk)
            qh, kh, ex = g_ref[d, 0, :, sl], g_ref[d, 1, :, sl], e_ref[d, :, sl]
            ch = dict(d=d, h=h, v=bf(g_ref[d, 2, :, sl]),
                      q_in=bf(qh * ex[0:c]), k_out=bf(kh * ex[c:2 * c]),
                      w_tot=ex[0:1] if d == 1 else ex[c - 1:c])
            att = masks[0] * mm_nt(bf(qh), bf(kh))
            for l in range(n_levels):
                lv = ex[(2 + l) * c:(3 + l) * c]
                att = att + masks[l + 1] * mm_nt(bf(qh * lv), bf(kh * lv))
            ch["att"] = bf(att)
            chains.append(ch)
            yield
    outs = [[None] * HGRN_HEADS for _ in range(2)]
    for ch in chains:
        st = s_ref[ch["d"], ch["h"]]
        outs[ch["d"]][ch["h"]] = mm(ch["att"], ch["v"]) + mm_nt(ch["q_in"], bf(st))
        s_ref[ch["d"], ch["h"]] = st * ch["w_tot"] + mm_tn(ch["v"], ch["k_out"])
    return [jnp.concatenate(o, axis=1) for o in outs]


def _shift_conv(prev_ref, cur_ref, nxt_ref, conv_ref, out_ref, is_lat, has_prev, has_next):
    n, width = cur_ref.shape
    row = lax.broadcasted_iota(jnp.int32, (n, LANES), 0)
    lat = jnp.where(is_lat, 1.0, 0.0)
    hp = jnp.where(has_prev, 1.0, 0.0)
    hn = jnp.where(has_next, 1.0, 0.0)
    edge_l = (1.0 - lat) * hp
    edge_r = (1.0 - lat) * hn
    for s in range(width // LANES):
        sl = slice(s * LANES, (s + 1) * LANES)
        kern = conv_ref[:, sl]
        prev, cur, nxt = prev_ref[:, sl], cur_ref[:, sl], nxt_ref[:, sl]
        k_up = kern[0:3] * (lat * hp)
        k_dn = kern[6:9] * (lat * hn)

        def column(j):
            return prev * k_up[j:j + 1] + cur * kern[3 + j:4 + j] + nxt * k_dn[j:j + 1]

        fill_l = prev[n - 1:n] * (edge_l * kern[3:4])
        fill_r = nxt[0:1] * (edge_r * kern[5:6])
        out_ref[:, sl] = (column(1)
                          + jnp.where(row == 0, fill_l, pltpu.roll(column(0), 1, 0))
                          + jnp.where(row == n - 1, fill_r, pltpu.roll(column(2), n - 1, 0)))
        if s == width // LANES - 1:
            yield


def _shift_kernel(prev_ref, cur_ref, nxt_ref, conv_ref, o_ref, *, ctx_chunks, nc, nl):
    tm = cur_ref.shape[0]
    cpt = tm // CHUNK
    first = pl.program_id(0) * cpt
    for c in range(cpt):
        g = first + c
        is_lat = g >= ctx_chunks
        pos = jnp.where(is_lat, (g - ctx_chunks) % nl, g % nc)
        has_prev = pos != 0
        has_next = pos != jnp.where(is_lat, nl - 1, nc - 1)
        prev = cur_ref.at[(c - 1) * CHUNK:c * CHUNK] if c else prev_ref
        nxt = cur_ref.at[(c + 1) * CHUNK:(c + 2) * CHUNK] if c < cpt - 1 else nxt_ref
        for _ in _shift_conv(prev, cur_ref.at[c * CHUNK:(c + 1) * CHUNK], nxt, conv_ref,
                             o_ref.at[c * CHUNK:(c + 1) * CHUNK], is_lat, has_prev, has_next):
            pass


def _shift_call(p_rw, conv, *, tm, n_batch, nc, nl):
    rows, n_rw = p_rw.shape
    cpt = tm // CHUNK
    last = rows // CHUNK - 1
    return pl.pallas_call(
        functools.partial(_shift_kernel, ctx_chunks=n_batch * nc, nc=nc, nl=nl),
        grid=(rows // tm,),
        in_specs=[pl.BlockSpec((CHUNK, n_rw), lambda i: (jnp.maximum(i * cpt - 1, 0), 0)),
                  pl.BlockSpec((tm, n_rw), lambda i: (i, 0)),
                  pl.BlockSpec((CHUNK, n_rw), lambda i: (jnp.minimum((i + 1) * cpt, last), 0)),
                  _resident(conv.shape)],
        out_specs=pl.BlockSpec((tm, n_rw), lambda i: (i, 0)),
        out_shape=jax.ShapeDtypeStruct((rows, n_rw), F32),
        compiler_params=_cparams("arbitrary"),
        name="token_shift",
    )(p_rw, p_rw, p_rw, conv)


def _pair_diag(x):
    lo = lax.broadcasted_iota(jnp.int32, x.shape, 1) < RWKV_DH
    z = jnp.zeros_like(x)
    return jnp.concatenate([jnp.where(lo, x, z), jnp.where(lo, z, x)], axis=0)


def _rwkv_scan_chunks(q_ref, s_ref):
    c = CHUNK
    pw = 2 * RWKV_DH
    n_pairs = q_ref.shape[3] // pw
    row = lax.broadcasted_iota(jnp.int32, (c, pw), 0)
    lane = lax.broadcasted_iota(jnp.int32, (c, pw), 1)
    col = lane & (RWKV_DH - 1)
    lo = lane < RWKV_DH
    eye = row == col
    bf = lambda t: t.astype(BF16)
    mm = lambda a, b: jnp.dot(a, b, preferred_element_type=F32)
    mm_nt = lambda a, b: lax.dot_general(a, b, (((1,), (1,)), ((), ())), preferred_element_type=F32)
    mm_tn = lambda a, b: lax.dot_general(a, b, (((0,), (0,)), ((), ())), preferred_element_type=F32)

    chains = []
    for d in range(2):
        rev = d == 1
        tri = (row <= col) if rev else (row >= col)
        strict = (row < col) if rev else (row > col)
        tri01 = jnp.where(tri, 1.0, 0.0).astype(BF16)
        for j in range(n_pairs):
            sl = slice(j * pw, (j + 1) * pw)
            r, lw, k, v, kk, a = (q_ref[d, t, :, sl] for t in range(6))
            b_incl = _dot01(tri01, lw)
            b_tot = b_incl[0:1] if rev else b_incl[c - 1:c]
            e_in = jnp.exp(-b_incl)
            e_out = jnp.exp(b_tot - b_incl)
            beta = kk * a
            chains.append(dict(
                d=d, j=j, tri=tri, strict=strict, w_tot=jnp.exp(b_tot), v=v,
                ar=bf(jnp.concatenate([-kk * jnp.exp(b_incl - lw), r * jnp.exp(b_incl)], axis=0)),
                bt=_pair_diag(bf(beta * e_in)), kt=_pair_diag(bf(k * e_in)),
                bk=bf(jnp.concatenate([beta * e_out, k * e_out], axis=0))))
    yield
    for ch in chains:
        g_b = mm_nt(ch["ar"], ch["bt"])
        g_k = mm_nt(ch["ar"], ch["kt"])
        a_ab = jnp.where(ch["strict"], g_b[0:c], 0.0)
        ch["a_rb"] = bf(jnp.where(ch["tri"], g_b[c:], 0.0))
        ch["a_k"] = bf(jnp.concatenate([jnp.where(ch["strict"], g_k[0:c], 0.0),
                                        jnp.where(ch["tri"], g_k[c:], 0.0)], axis=0))
        ch["p"] = jnp.where(eye, 1.0, a_ab)
        ch["ak"] = bf(a_ab)
    yield
    for ch in chains:
        ch["ak2"] = mm(ch["ak"], _pair_diag(ch["ak"]))
    yield
    n_lv = int(math.log2(c))
    for lv in range(2, n_lv + 1):
        for ch in chains:
            ak = bf(ch["ak2"])
            p_bd = _pair_diag(bf(ch["p"]))
            if lv < n_lv:
                both = mm(ak, jnp.concatenate([_pair_diag(ak), p_bd], axis=1))
                ch["ak2"] = both[:, :pw]
                ch["p"] = ch["p"] + both[:, pw:]
            else:
                ch["p"] = ch["p"] + mm(ak, p_bd)
        yield
    for ch in chains:
        s0 = s_ref[ch["d"], ch["j"]]
        ch["s0"] = s0
        ch["xs"] = mm_nt(ch["ar"], _pair_diag(bf(s0)))
        ch["av"] = mm(ch["a_k"], _pair_diag(bf(ch["v"])))
    yield
    for ch in chains:
        x = ch["xs"][0:c] + ch["av"][0:c]
        ch["u"] = mm(bf(ch["p"]), _pair_diag(bf(x)))
    yield
    outs = [[None] * n_pairs for _ in range(2)]
    for ch in chains:
        u = ch["u"]
        outs[ch["d"]][ch["j"]] = ch["xs"][c:] + ch["av"][c:] + mm(ch["a_rb"], _pair_diag(bf(u)))
        z = mm_tn(bf(jnp.concatenate([u, ch["v"]], axis=0)), ch["bk"])
        s_ref[ch["d"], ch["j"]] = ch["s0"] * ch["w_tot"] + jnp.where(lo, z[0:c], z[c:])
    return [jnp.concatenate(o, axis=1) for o in outs]


def _rwkv_prep(rw_ref, vec_ref, wup_ref, aup_ref, gup_ref, bd_ref, q_ref, bonus_ref, gr_ref, d):
    dr = q_ref.shape[3]
    o_w = 3 * dr
    o_a = o_w + 2 * W_RANK
    o_g = o_a + 2 * A_RANK
    bd = bd_ref[0:LANES, 0:LANES]
    tanh_wd = jnp.tanh(rw_ref[:, o_w + d * W_RANK:o_w + (d + 1) * W_RANK]).astype(BF16)
    ad = [rw_ref[:, o_a + e * A_RANK:o_a + (e + 1) * A_RANK].astype(BF16) for e in range(2)]
    if d == 0:
        sig_gd = _sigmoid(rw_ref[:, o_g:o_g + G_RANK]).astype(BF16)
    mm = lambda a, b: jnp.dot(a, b, preferred_element_type=F32)
    for j in range(dr // LANES):
        sl = slice(j * LANES, (j + 1) * LANES)
        vec = vec_ref[:, sl]
        w0, a0 = vec[0:2], vec[2:4]
        k_k, k_a, r_k = vec[4:5], vec[5:6], vec[6:7]
        r = rw_ref[:, j * LANES:(j + 1) * LANES]
        k = rw_ref[:, dr + j * LANES:dr + (j + 1) * LANES]
        v = rw_ref[:, 2 * dr + j * LANES:2 * dr + (j + 1) * LANES]
        lw = -math.exp(-0.5) * _sigmoid(w0[d:d + 1] + mm(tanh_wd, wup_ref[d, :, sl]))
        a = _sigmoid(a0[d:d + 1] + mm(ad[d], aup_ref[d, :, sl]))
        kk = k * k_k
        kk = kk / jnp.maximum(jnp.sqrt(_dot_x01(kk * kk, bd)), 1e-12)
        for t, val in enumerate((r, lw, k * (1.0 + (a - 1.0) * k_a), v, kk, a)):
            q_ref[d, t, :, sl] = val
        if d == 0:
            a_b = _sigmoid(a0[1:2] + mm(ad[1], aup_ref[1, :, sl]))
            k_sum = k * (2.0 + (a + a_b - 2.0) * k_a)
            bonus_ref[:, sl] = (_dot_x01(r * k_sum * r_k, bd) * v).astype(bonus_ref.dtype)
            gr_ref[:, sl] = mm(sig_gd, gup_ref[:, sl]).astype(gr_ref.dtype)
        if j % 2 == 1:
            yield


def _mix_kernel(qf_ref, zf_ref, vf_ref, qb_ref, zb_ref, vb_ref, lbp_ref, tab_ref, mask_ref,
                rwf_ref, rwb_ref, vec_ref, wup_ref, aup_ref, gup_ref,
                bd_ref, of_ref, ob_ref, yf_ref, yb_ref, bonus_ref, gr_ref,
                hs_ref, g_ref, e_ref, s_ref, q_ref):
    step = pl.program_id(1)

    @pl.when(step == 0)
    def _():
        g_ref[...] = jnp.zeros(g_ref.shape, F32)
        e_ref[...] = jnp.zeros(e_ref.shape, F32)
        q_ref[...] = jnp.zeros(q_ref.shape, F32)

    @pl.when(step <= 1)
    def _():
        hs_ref[...] = jnp.zeros(hs_ref.shape, F32)
        s_ref[...] = jnp.zeros(s_ref.shape, F32)

    scan = _rwkv_scan_chunks(q_ref, s_ref)
    next(scan)

    def prepare():
        for d, rw_ref in enumerate((rwf_ref, rwb_ref)):
            yield from _rwkv_prep(rw_ref, vec_ref, wup_ref, aup_ref, gup_ref, bd_ref, q_ref, bonus_ref,
                                  gr_ref, d)

    (y_f, y_b), _, (o_f, o_b), _ = _interleave(
        scan, prepare(), _gla_scan(g_ref, e_ref, mask_ref, hs_ref),
        _gla_gates(((qf_ref, zf_ref, vf_ref), (qb_ref, zb_ref, vb_ref)), lbp_ref, tab_ref, g_ref, e_ref))
    for ref, val in ((yf_ref, y_f), (yb_ref, y_b), (of_ref, o_f), (ob_ref, o_b)):
        ref[...] = val.astype(ref.dtype)


def _mix_call(p_hg, p_rw, lbp, vec, w_up, a_up, g_up, bd, *, n_batch, nc, nl):
    rows, n_rw = p_rw.shape
    dh = p_hg.shape[1] // 5
    dr = bd.shape[0]
    nt = nc + nl
    rb, c_fwd, c_bwd = _chunk_maps(n_batch, nc, nl)
    tab, masks = _gla_tables()
    tab = jnp.asarray(tab, BF16)
    masks = jnp.asarray(masks, F32)
    nxt = lambda i: jnp.minimum(i, nt - 1)
    cur = lambda i: jnp.maximum(i - 1, 0)

    def hg(col, cmap):
        return pl.BlockSpec((CHUNK, dh), lambda b, i: (rb(b, cmap(nxt(i))), col))

    def rw(cmap):
        return pl.BlockSpec((CHUNK, n_rw), lambda b, i: (rb(b, cmap(nxt(i))), 0))

    def out(width, cmap, when):
        return pl.BlockSpec((CHUNK, width), lambda b, i: (rb(b, cmap(when(i))), 0))

    dk = dh // HGRN_HEADS
    n_heads = dr // RWKV_DH
    consts = (lbp, tab, masks)
    consts2 = (vec, w_up, a_up, g_up, bd)
    return pl.pallas_call(
        _mix_kernel,
        grid=(n_batch, nt + 1),
        in_specs=[hg(0, c_fwd), hg(1, c_fwd), hg(3, c_fwd), hg(0, c_bwd), hg(2, c_bwd), hg(3, c_bwd)]
        + [_resident(a.shape) for a in consts]
        + [rw(c_fwd), rw(c_bwd)]
        + [_resident(a.shape) for a in consts2],
        out_specs=[out(dh, c_fwd, cur), out(dh, c_bwd, cur), out(dr, c_fwd, cur), out(dr, c_bwd, cur),
                   out(dr, c_fwd, nxt), out(dr, c_fwd, nxt)],
        out_shape=[jax.ShapeDtypeStruct((rows, dh), BF16)] * 2 + [jax.ShapeDtypeStruct((rows, dr), BF16)] * 4,
        scratch_shapes=[pltpu.VMEM((2, HGRN_HEADS, dk, dk), F32),
                        pltpu.VMEM((2, 3, CHUNK, dh), F32),
                        pltpu.VMEM((2, tab.shape[1], dh), F32),
                        pltpu.VMEM((2, n_heads // 2, RWKV_DH, 2 * RWKV_DH), F32),
                        pltpu.VMEM((2, 6, CHUNK, dr), F32)],
        compiler_params=_cparams("arbitrary", "arbitrary"),
        name="token_mix_scans",
    )(p_hg, p_hg, p_hg, p_hg, p_hg, p_hg, *consts, p_rw, p_rw, *consts2)


def _post_kernel(x_ref, m_ref, of_ref, ob_ref, go_ref, yf_ref, yb_ref, bonus_ref, gr_ref, vec_ref,
                 bd_ref, wo_ref, o_ref):
    vec = vec_ref[...]
    o_gain, ln_w, ln_b = vec[0:1], vec[1:2], vec[2:3]
    f32 = lambda ref: ref[...].astype(F32)
    o = f32(of_ref) + f32(ob_ref)
    dh = o.shape[1] // HGRN_HEADS
    parts = []
    for h in range(HGRN_HEADS):
        oh = o[:, h * dh:(h + 1) * dh]
        parts.append(oh * lax.rsqrt(jnp.mean(oh * oh, axis=-1, keepdims=True) + RMS_EPS))
    o = jnp.concatenate(parts, axis=1) * o_gain * _silu(go_ref[...])
    y = f32(yf_ref) + f32(yb_ref)
    bd = bd_ref[...]
    inv = 1.0 / RWKV_DH
    mu = _dot_x01(y, bd) * inv
    yc = y - mu
    var = _dot_x01(yc * yc, bd) * inv
    y = yc * lax.rsqrt(var + GN_EPS) * ln_w + ln_b
    y = (y + f32(bonus_ref)) * f32(gr_ref)
    mixed = jnp.concatenate([o, y], axis=1).astype(BF16)
    out = jnp.dot(mixed, wo_ref[...], preferred_element_type=F32)
    o_ref[...] = x_ref[...] + m_ref[0][5:6] * out


def _post_call(xa, mods, o_f, o_b, p_hg, y_f, y_b, bonus, g_r, vec, bd, w_out, layer, *, tm, ctx_tiles,
               tiles_per_batch, n_batch):
    rows, d = xa.shape
    dh = o_f.shape[1]
    dr = y_f.shape[1]
    row = lambda w: pl.BlockSpec((tm, w), lambda i: (i, 0))
    return pl.pallas_call(
        _post_kernel,
        grid=(rows // tm,),
        in_specs=[row(d),
                  pl.BlockSpec((1, N_MOD, d),
                               lambda i: (_mod_index(i, ctx_tiles, tiles_per_batch, n_batch), 0, 0)),
                  row(dh), row(dh), pl.BlockSpec((tm, dh), lambda i: (i, 4)),
                  row(dr), row(dr), row(dr), row(dr),
                  _resident(vec.shape), _resident(bd.shape), _layer_block(w_out.shape[1:], layer)],
        out_specs=row(d),
        out_shape=jax.ShapeDtypeStruct((rows, d), F32),
        compiler_params=_cparams("arbitrary"),
        name="mix_out_projection",
    )(xa, mods, o_f, o_b, p_hg, y_f, y_b, bonus, g_r, vec, bd, w_out)


def kernel(x, c, ctx, c_ctx, w_mod, b_mod, norm_gains, final_gain, ffn1_up, ffn1_down, ffn2_up, ffn2_down, w_in, w_out, hgrn_lb_logits, hgrn_o_gain, rwkv_conv, rwkv_w0, rwkv_w_up, rwkv_a0, rwkv_a_up, rwkv_g_up, rwkv_k_k, rwkv_k_a, rwkv_r_k, rwkv_ln_w, rwkv_ln_b):
    n_batch, seq, d = x.shape
    n_ctx = ctx.shape[1]
    depth = w_mod.shape[0]
    d_hgrn = hgrn_o_gain.shape[1]
    d_rwkv = rwkv_k_k.shape[1]
    hg_cols = 5 * d_hgrn
    assert seq % CHUNK == 0 and n_ctx % CHUNK == 0 and GRID_W == CHUNK and n_batch < 8
    nc, nl = n_ctx // CHUNK, seq // CHUNK

    p = jax.nn.softmax(hgrn_lb_logits.astype(F32), axis=0)
    cum = jnp.cumsum(p, axis=0)
    lb = cum - cum[0]
    lbp = jnp.stack([jnp.log(lb), jnp.log1p(-lb), 1.0 - lb], axis=2)

    cvec = jnp.concatenate([c, c_ctx[None], jnp.zeros((7 - n_batch, d), F32)], axis=0)
    mods = _mod_call(cvec, w_mod, b_mod).reshape(depth, 8, N_MOD, d)

    bd = jnp.asarray(np.kron(np.eye(d_rwkv // RWKV_DH), np.ones((RWKV_DH, RWKV_DH))), BF16)

    xa = (ctx.reshape(n_batch * n_ctx, d), x.reshape(n_batch * seq, d))
    tm = _row_tile(n_batch * n_ctx, seq, 1024)
    tm_s = _row_tile(n_batch * n_ctx, seq, 512)
    tiles = dict(tm=tm, ctx_tiles=n_batch * n_ctx // tm, tiles_per_batch=seq // tm, n_batch=n_batch)
    tiles_s = dict(tm=tm_s, ctx_tiles=n_batch * n_ctx // tm_s, tiles_per_batch=seq // tm_s, n_batch=n_batch)

    bf = lambda w: w.astype(BF16)
    up1, down1, up2, down2 = bf(ffn1_up), bf(ffn1_down), bf(ffn2_up), bf(ffn2_down)
    w_in_b, w_out_b = bf(w_in), bf(w_out)
    for l in range(depth):
        last = l == depth - 1
        xa = _ffn_call(xa, mods[l], norm_gains[l, 0], up1, down1, l, mod_base=0, **tiles)
        p_hg, p_rw = _inproj_call(xa, mods[l], norm_gains[l, 1], w_in_b, l, hg_cols, **tiles_s)
        p_rw = _shift_call(p_rw, rwkv_conv[l].reshape(9, -1), tm=tm_s, n_batch=n_batch, nc=nc, nl=nl)
        vec = jnp.concatenate([rwkv_w0[l], rwkv_a0[l], rwkv_k_k[l][None], rwkv_k_a[l][None],
                               rwkv_r_k[l].reshape(1, d_rwkv), jnp.zeros((1, d_rwkv), F32)], axis=0)
        o_f, o_b, y_f, y_b, bonus, g_r = _mix_call(
            p_hg, p_rw, lbp[l], vec, bf(rwkv_w_up[l]), bf(rwkv_a_up[l]), bf(rwkv_g_up[l]), bd,
            n_batch=n_batch, nc=nc, nl=nl)
        vec2 = jnp.concatenate([hgrn_o_gain[l][None], rwkv_ln_w[l][None], rwkv_ln_b[l][None],
                                jnp.zeros((5, d_rwkv), F32)], axis=0)
        xa = _post_call(xa, mods[l], o_f, o_b, p_hg, y_f, y_b, bonus, g_r, vec2, bd, w_out_b, l, **tiles)
        xa = _ffn_call(xa, mods[l], norm_gains[l, 2], up2, down2, l, mod_base=6,
                       skip_ctx=last, final_gain=final_gain if last else None, **tiles)
    return xa.reshape(n_batch, seq, d)
```
